```python
import math
import jax, jax.numpy as jnp
from jax import lax
import numpy as np

D_MODEL = 1024
BATCH = 4
SEQ = 4096
DEPTH = 2

N_BRANCH = 4
D_MIX = D_MODEL // N_BRANCH
HEAD_A = 64
H_A = D_MIX // HEAD_A
LORA_W = 32
LORA_A = 32
LORA_V = 32
LORA_G = 64
LNX_EPS = 64e-5
CHUNK = 128
GROUP_B = 64
G_B = D_MIX // GROUP_B
CONV_K = 3
GROUP_D = 16
G_D = D_MIX // GROUP_D
N_STATE = 64
D_FF = 4 * D_MODEL
EPS = 1e-6
LN_EPS = 1e-5

A_COLS = 3 * D_MIX + LORA_W + LORA_A + LORA_G
B_COLS = 2 * D_MIX
C_COLS = 3 * D_MIX
D_COLS = D_MIX
GATE_COLS = N_BRANCH * D_MODEL
OFF_B = A_COLS
OFF_C = OFF_B + B_COLS
OFF_D = OFF_C + C_COLS
OFF_G = OFF_D + D_COLS
N_IN = OFF_G + GATE_COLS

kernel_name = "hybrid_rwkv7_sgu_conv_s5_block"


def rms_norm(x, g):
    xf = x.astype(jnp.float32)
    return xf * lax.rsqrt(jnp.mean(xf * xf, axis=-1, keepdims=True) + EPS) * g.astype(jnp.float32)


def layer_norm(x, w, b, eps):
    xf = x.astype(jnp.float32)
    mu = jnp.mean(xf, axis=-1, keepdims=True)
    var = jnp.mean(jnp.square(xf - mu), axis=-1, keepdims=True)
    return (xf - mu) * lax.rsqrt(var + eps) * w.astype(jnp.float32) + b.astype(jnp.float32)


def shift_prev(p):
    return jnp.pad(p, ((0, 0), (1, 0), (0, 0)))[:, :-1]


def wkv7(r, w, k, v, a, b):
    bsz, _, nh, n = r.shape

    def step(state, inp):
        r_t, w_t, k_t, v_t, a_t, b_t = inp
        sa = jnp.einsum('bhij,bhj->bhi', state, a_t)
        state = (state * w_t[:, :, None, :] + sa[..., None] * b_t[:, :, None, :]
                 + v_t[..., None] * k_t[:, :, None, :])
        return state, jnp.einsum('bhij,bhj->bhi', state, r_t)

    xs = tuple(jnp.moveaxis(t, 1, 0) for t in (r, w, k, v, a, b))
    s0 = jnp.zeros((bsz, nh, n, n), jnp.float32)
    _, out = lax.scan(step, s0, xs)
    return jnp.moveaxis(out, 0, 1)


def rwkv7_mixer(p, v_first, v_mix, mu, w0, w2, a0, a2, g2, k_k, k_a, r_k, lnx_w, lnx_b, w_out):
    bsz, seq, _ = p.shape
    p = p + (shift_prev(p) - p) * mu
    r = p[..., :D_MIX]
    k = p[..., D_MIX:2 * D_MIX]
    v = p[..., 2 * D_MIX:3 * D_MIX]
    wd = p[..., 3 * D_MIX:3 * D_MIX + LORA_W]
    ad = p[..., 3 * D_MIX + LORA_W:3 * D_MIX + LORA_W + LORA_A]
    gd = p[..., 3 * D_MIX + LORA_W + LORA_A:]
    log_w = -jax.nn.softplus(-(w0 + jnp.tanh(wd) @ w2)) - 0.5
    decay = jnp.exp(-jnp.exp(log_w))
    if v_mix is None:
        v_first = v
    else:
        v0, v1, v2 = v_mix
        v = v + (v_first - v) * jax.nn.sigmoid(v0 + (v @ v1) @ v2)
    a = jax.nn.sigmoid(a0 + ad @ a2)
    g = jax.nn.sigmoid(gd) @ g2

    def heads(t):
        return t.reshape(bsz, seq, H_A, HEAD_A).astype(jnp.float32)

    kk = heads(k * k_k)
    kk = kk / jnp.maximum(jnp.sqrt(jnp.sum(kk * kk, axis=-1, keepdims=True)), 1e-12)
    k = k * (1.0 + (a - 1.0) * k_a)
    rh, kh, vh, ah = heads(r), heads(k), heads(v), heads(a)
    o = wkv7(rh, heads(decay), kh, vh, -kk, kk * ah)
    o = layer_norm(o, lnx_w.reshape(H_A, HEAD_A), lnx_b.reshape(H_A, HEAD_A), LNX_EPS)
    o = o + jnp.sum(rh * kh * r_k, axis=-1, keepdims=True) * vh
    return (o.reshape(bsz, seq, D_MIX) * g) @ w_out, v_first


def spatial_gating_mixer(p, ln_w, ln_b, w_s, b_s, w_out):
    bsz, seq, _ = p.shape
    z = jax.nn.gelu(p)
    u = z[..., :D_MIX]
    v = layer_norm(z[..., D_MIX:], ln_w, ln_b, LN_EPS)
    v = v.reshape(bsz, seq // CHUNK, CHUNK, G_B, GROUP_B)
    mask = jnp.tril(jnp.ones((CHUNK, CHUNK), jnp.float32))
    mixed = jnp.einsum('gts,bnsgc->bntgc', w_s * mask, v) + b_s.T[None, None, :, :, None]
    return (u * mixed.reshape(bsz, seq, D_MIX)) @ w_out


def short_conv_mixer(p, conv_w, w_out):
    seq = p.shape[1]
    bg = p[..., :D_MIX]
    cg = p[..., D_MIX:2 * D_MIX]
    xin = p[..., 2 * D_MIX:]
    zp = jnp.pad(cg * xin, ((0, 0), (CONV_K - 1, 0), (0, 0)))
    y = sum(conv_w[j] * zp[:, j:j + seq] for j in range(CONV_K))
    return (bg * y) @ w_out


def _complex_affine_combine(e1, e2):
    a1r, a1i, b1r, b1i = e1
    a2r, a2i, b2r, b2i = e2
    return (a2r * a1r - a2i * a1i,
            a2r * a1i + a2i * a1r,
            a2r * b1r - a2i * b1i + b2r,
            a2r * b1i + a2i * b1r + b2i)


def s5_mixer(u, a_re, a_im, b_re, b_im, c_re, c_im, d, log_dt, glu_w):
    f32 = jnp.float32
    bsz, seq, _ = u.shape
    u32 = u.astype(f32)
    ug = u32.reshape(bsz, seq, G_D, GROUP_D)
    lam_re = jnp.minimum(a_re.astype(f32), -1e-4)
    lam_im = a_im.astype(f32)
    dt = jnp.exp(log_dt.astype(f32))[:, None]
    mag = jnp.exp(lam_re * dt)
    ab_re = mag * jnp.cos(lam_im * dt)
    ab_im = mag * jnp.sin(lam_im * dt)
    den = lam_re * lam_re + lam_im * lam_im
    q_re = ((ab_re - 1.0) * lam_re + ab_im * lam_im) / den
    q_im = (ab_im * lam_re - (ab_re - 1.0) * lam_im) / den
    bb_re = q_re[..., None] * b_re - q_im[..., None] * b_im
    bb_im = q_re[..., None] * b_im + q_im[..., None] * b_re
    bu_re = jnp.einsum('gnc,bsgc->bsgn', bb_re, ug)
    bu_im = jnp.einsum('gnc,bsgc->bsgn', bb_im, ug)
    ar = jnp.broadcast_to(ab_re, bu_re.shape)
    ai = jnp.broadcast_to(ab_im, bu_im.shape)
    _, _, x_re, x_im = lax.associative_scan(_complex_affine_combine, (ar, ai, bu_re, bu_im), axis=1)
    y = (jnp.einsum('gcn,bsgn->bsgc', c_re, x_re) - jnp.einsum('gcn,bsgn->bsgc', c_im, x_im))
    y = y.reshape(bsz, seq, D_MIX) + d * u32
    h = jax.nn.gelu(y) @ glu_w
    return h[..., :D_MODEL] * jax.nn.sigmoid(h[..., D_MODEL:])


def setup_inputs(seed: int = 0) -> dict:
    key = jax.random.key(seed)
    ks = iter(jax.random.split(key, 64))
    f32 = jnp.float32
    L = DEPTH

    def nrm(shape, scale):
        return scale * jax.random.normal(next(ks), shape, f32)

    def uni(shape, lo, hi):
        return jax.random.uniform(next(ks), shape, f32, lo, hi)

    return {
        'x': nrm((BATCH, SEQ, D_MODEL), 1.0),
        'c': nrm((BATCH, D_MODEL), 1.0),
        'ada_w': nrm((L, D_MODEL, 6 * D_MODEL), 0.5 * D_MODEL ** -0.5),
        'ada_b': nrm((L, 6 * D_MODEL), 0.02),
        'norm_mix_g': 1.0 + nrm((L, D_MODEL), 0.1),
        'w_in': nrm((L, D_MODEL, N_IN), D_MODEL ** -0.5),
        'rwkv_mu': uni((L, A_COLS), 0.0, 1.0),
        'rwkv_w0': uni((L, D_MIX), -6.0, -1.0),
        'rwkv_w2': nrm((L, LORA_W, D_MIX), 0.1),
        'rwkv_a0': nrm((L, D_MIX), 0.1),
        'rwkv_a2': nrm((L, LORA_A, D_MIX), 0.1),
        'rwkv_g2': nrm((L, LORA_G, D_MIX), LORA_G ** -0.5),
        'rwkv_v0': 1.0 + nrm((L - 1, D_MIX), 0.1),
        'rwkv_v1': nrm((L - 1, D_MIX, LORA_V), D_MIX ** -0.5),
        'rwkv_v2': nrm((L - 1, LORA_V, D_MIX), 0.1),
        'rwkv_kk': 0.85 + nrm((L, D_MIX), 0.1),
        'rwkv_ka': 1.0 + nrm((L, D_MIX), 0.1),
        'rwkv_rk': nrm((L, H_A, HEAD_A), 0.1),
        'rwkv_lnx_w': 1.0 + nrm((L, D_MIX), 0.1),
        'rwkv_lnx_b': nrm((L, D_MIX), 0.02),
        'rwkv_out': nrm((L, D_MIX, D_MODEL), D_MIX ** -0.5),
        'sg_ln_w': 1.0 + nrm((L, D_MIX), 0.1),
        'sg_ln_b': nrm((L, D_MIX), 0.02),
        'sg_ws': nrm((L, G_B, CHUNK, CHUNK), CHUNK ** -0.5),
        'sg_bs': 1.0 + nrm((L, G_B, CHUNK), 0.1),
        'sg_out': nrm((L, D_MIX, D_MODEL), D_MIX ** -0.5),
        'conv_w': nrm((L, CONV_K, D_MIX), CONV_K ** -0.5),
        'conv_out': nrm((L, D_MIX, D_MODEL), D_MIX ** -0.5),
        's5_a_re': -0.5 + nrm((L, G_D, N_STATE), 0.01),
        's5_a_im': jnp.pi * jnp.arange(N_STATE, dtype=f32)[None, None, :] + nrm((L, G_D, N_STATE), 0.01),
        's5_b_re': nrm((L, G_D, N_STATE, GROUP_D), (2 * GROUP_D) ** -0.5),
        's5_b_im': nrm((L, G_D, N_STATE, GROUP_D), (2 * GROUP_D) ** -0.5),
        's5_c_re': nrm((L, G_D, GROUP_D, N_STATE), (2 * N_STATE) ** -0.5),
        's5_c_im': nrm((L, G_D, GROUP_D, N_STATE), (2 * N_STATE) ** -0.5),
        's5_d': nrm((L, D_MIX), 1.0),
        's5_log_dt': uni((L, G_D), math.log(1e-3), math.log(1e-1)),
        's5_glu_w': nrm((L, D_MIX, 2 * D_MODEL), D_MIX ** -0.5),
        'w_o': nrm((L, D_MODEL, D_MODEL), D_MODEL ** -0.5),
        'norm_ffn_g': 1.0 + nrm((L, D_MODEL), 0.1),
        'ffn_w1': nrm((L, D_MODEL, D_FF), D_MODEL ** -0.5),
        'ffn_w2': nrm((L, D_FF, D_MODEL), D_FF ** -0.5),
        'final_g': 1.0 + nrm((D_MODEL,), 0.1),
    }


def reference(x, c, ada_w, ada_b, norm_mix_g, w_in, rwkv_mu, rwkv_w0, rwkv_w2, rwkv_a0, rwkv_a2,
              rwkv_g2, rwkv_v0, rwkv_v1, rwkv_v2, rwkv_kk, rwkv_ka, rwkv_rk, rwkv_lnx_w, rwkv_lnx_b,
              rwkv_out, sg_ln_w, sg_ln_b, sg_ws, sg_bs, sg_out, conv_w, conv_out, s5_a_re, s5_a_im,
              s5_b_re, s5_b_im, s5_c_re, s5_c_im, s5_d, s5_log_dt, s5_glu_w, w_o, norm_ffn_g,
              ffn_w1, ffn_w2, final_g):
    in_dtype = x.dtype
    bsz, seq, _ = x.shape
    c_act = jax.nn.silu(c.astype(jnp.float32))
    v_first = None
    for l in range(DEPTH):
        mod = (c_act @ ada_w[l] + ada_b[l])[:, None, :]
        sh1, sc1, gt1, sh2, sc2, gt2 = jnp.split(mod, 6, axis=-1)

        h = rms_norm(x, norm_mix_g[l]) * (1.0 + sc1) + sh1
        p = h @ w_in[l]
        v_mix = None if l == 0 else (rwkv_v0[l - 1], rwkv_v1[l - 1], rwkv_v2[l - 1])
        y_a, v_first = rwkv7_mixer(p[..., :OFF_B], v_first, v_mix, rwkv_mu[l], rwkv_w0[l], rwkv_w2[l],
                                   rwkv_a0[l], rwkv_a2[l], rwkv_g2[l], rwkv_kk[l], rwkv_ka[l],
                                   rwkv_rk[l], rwkv_lnx_w[l], rwkv_lnx_b[l], rwkv_out[l])
        y_b = spatial_gating_mixer(p[..., OFF_B:OFF_C], sg_ln_w[l], sg_ln_b[l], sg_ws[l], sg_bs[l], sg_out[l])
        y_c = short_conv_mixer(p[..., OFF_C:OFF_D], conv_w[l], conv_out[l])
        y_d = s5_mixer(p[..., OFF_D:OFF_G], s5_a_re[l], s5_a_im[l], s5_b_re[l], s5_b_im[l],
                       s5_c_re[l], s5_c_im[l], s5_d[l], s5_log_dt[l], s5_glu_w[l])
        gates = jax.nn.sigmoid(p[..., OFF_G:]).reshape(bsz, seq, N_BRANCH, D_MODEL)
        merged = (gates[:, :, 0] * y_a + gates[:, :, 1] * y_b
                  + gates[:, :, 2] * y_c + gates[:, :, 3] * y_d)
        x = x + gt1 * (merged @ w_o[l])

        h = rms_norm(x, norm_ffn_g[l]) * (1.0 + sc2) + sh2
        x = x + gt2 * (jnp.square(jax.nn.relu(h @ ffn_w1[l])) @ ffn_w2[l])
    return rms_norm(x, final_g).astype(in_dtype)
```

```python
import functools
import math

import jax
import jax.numpy as jnp
from jax import lax
from jax.experimental import pallas as pl
from jax.experimental.pallas import tpu as pltpu

F32 = jnp.float32
BF16 = jnp.bfloat16

D_MODEL = 1024
N_BRANCH = 4
D_MIX = D_MODEL // N_BRANCH
HEAD_A = 64
H_A = D_MIX // HEAD_A
LORA_W = 32
LORA_A = 32
LORA_G = 64
LNX_EPS = 64e-5
CHUNK = 128
GROUP_B = 64
G_B = D_MIX // GROUP_B
CONV_K = 3
GROUP_D = 16
G_D = D_MIX // GROUP_D
N_STATE = 64
D_FF = 4 * D_MODEL
EPS = 1e-6
LN_EPS = 1e-5

A_COLS = 3 * D_MIX + LORA_W + LORA_A + LORA_G
B_COLS = 2 * D_MIX
C_COLS = 3 * D_MIX
D_COLS = D_MIX
OFF_B = A_COLS
OFF_C = OFF_B + B_COLS
OFF_D = OFF_C + C_COLS
OFF_G = OFF_D + D_COLS
LORA_COLS = LORA_W + LORA_A + LORA_G

WKV_CHUNK = 64
S5_LANES = 2 * G_D * N_STATE
S5_SUBSEQ = 8
RW_COLS = 8 * D_MIX

VMEM_LIMIT = 56 * 1024 * 1024


def _cparams(sem):
    return pltpu.CompilerParams(dimension_semantics=sem, vmem_limit_bytes=VMEM_LIMIT)


def _dot(a, b):
    return jnp.dot(a.astype(BF16), b.astype(BF16), preferred_element_type=F32)


def _dot_nt(a, b):
    return lax.dot_general(a.astype(BF16), b.astype(BF16), (((1,), (1,)), ((), ())),
                           preferred_element_type=F32)


def _dot_tn(a, b):
    return lax.dot_general(a.astype(BF16), b.astype(BF16), (((0,), (0,)), ((), ())),
                           preferred_element_type=F32)


def _split3(x):
    hi = x.astype(BF16)
    r1 = x - hi.astype(F32)
    mid = r1.astype(BF16)
    lo = (r1 - mid.astype(F32)).astype(BF16)
    return hi, mid, lo


def _dot_sel(sel, x):
    hi, mid, lo = _split3(x)
    return (jnp.dot(sel, hi, preferred_element_type=F32) + jnp.dot(sel, mid, preferred_element_type=F32)
            + jnp.dot(sel, lo, preferred_element_type=F32))


def _dot_x_sel(x, sel):
    hi, mid, lo = _split3(x)
    return (jnp.dot(hi, sel, preferred_element_type=F32) + jnp.dot(mid, sel, preferred_element_type=F32)
            + jnp.dot(lo, sel, preferred_element_type=F32))


def _dot3(a, b):
    ah, am, _ = _split3(a)
    bh, bm, _ = _split3(b)
    return (jnp.dot(ah, bh, preferred_element_type=F32) + jnp.dot(ah, bm, preferred_element_type=F32)
            + jnp.dot(am, bh, preferred_element_type=F32))


def _sigmoid(x):
    return 1.0 / (1.0 + jnp.exp(-x))


def _gelu_tanh(x):
    return 0.5 * x * (1.0 + jnp.tanh(math.sqrt(2.0 / math.pi) * (x + 0.044715 * (x * x * x))))


def _rms_mod(x, g, sc, sh):
    ms = jnp.mean(x * x, axis=-1, keepdims=True)
    return x * lax.rsqrt(ms + EPS) * g * (1.0 + sc) + sh


def _ada_kernel(c_ref, w_ref, b_ref, o_ref):
    c = c_ref[...]
    ca = c * _sigmoid(c)
    o_ref[0] = _dot3(ca, w_ref[0]) + b_ref[0]


def _ada_call(c_pad, ada_w, ada_b):
    depth, d, n = ada_w.shape
    rows = c_pad.shape[0]
    bn = 1536
    return pl.pallas_call(
        _ada_kernel,
        grid=(depth, n // bn),
        in_specs=[
            pl.BlockSpec((rows, d), lambda l, j: (0, 0)),
            pl.BlockSpec((1, d, bn), lambda l, j: (l, 0, j)),
            pl.BlockSpec((1, 1, bn), lambda l, j: (l, 0, j)),
        ],
        out_specs=pl.BlockSpec((1, rows, bn), lambda l, j: (l, 0, j)),
        out_shape=jax.ShapeDtypeStruct((depth, rows, n), F32),
        compiler_params=_cparams(("parallel", "parallel")),
        name="ada_mod",
    )(c_pad, ada_w, ada_b.reshape(depth, 1, n))


def _front_kernel(has_vmix, tt, *refs):
    if has_vmix:
        (x_ref, mod_ref, ng_ref, win_ref, mu_ref, w0_ref, w2_ref, a0_ref, a2_ref, g2_ref, kk_ref,
         ka_ref, rk_ref, seg_ref, lnw_ref, lnb_ref, ws_ref, sgb_ref, cw_ref,
         vf_ref, v0_ref, v1_ref, v2_ref,
         rw_ref, fb_ref, fc_ref, u5_ref, pa_s, z_s) = refs
    else:
        (x_ref, mod_ref, ng_ref, win_ref, mu_ref, w0_ref, w2_ref, a0_ref, a2_ref, g2_ref, kk_ref,
         ka_ref, rk_ref, seg_ref, lnw_ref, lnb_ref, ws_ref, sgb_ref, cw_ref,
         rw_ref, fb_ref, fc_ref, u5_ref, pa_s, z_s) = refs

    t_idx = pl.program_id(1)

    @pl.when(t_idx == 0)
    def _():
        pa_s[0:8, :] = jnp.zeros((8, A_COLS), F32)
        z_s[0:8, :] = jnp.zeros((8, D_MIX), F32)

    mod = mod_ref[0]
    h = _rms_mod(x_ref[0], ng_ref[...], mod[1:2, :], mod[0:1, :])
    p = jnp.dot(h.astype(BF16), win_ref[...], preferred_element_type=F32)

    pa_s[8:8 + tt, :] = p[:, :A_COLS]
    pa = p[:, :A_COLS]
    prev = pa_s[7:7 + tt, :]
    pa = pa + (prev - pa) * mu_ref[...]
    pa_s[0:8, :] = pa_s[tt:tt + 8, :]
    r = pa[:, 0:D_MIX]
    k = pa[:, D_MIX:2 * D_MIX]
    v = pa[:, 2 * D_MIX:3 * D_MIX]
    lora = pa[:, 3 * D_MIX:A_COLS]
    lw = -math.exp(-0.5) * _sigmoid(w0_ref[...] + _dot(jnp.tanh(lora), w2_ref[...]))
    if has_vmix:
        vgate = _sigmoid(v0_ref[...] + _dot(_dot(v, v1_ref[...]), v2_ref[...]))
        v = v + (vf_ref[0] - v) * vgate
    a = _sigmoid(a0_ref[...] + _dot(lora, a2_ref[...]))
    g = _dot(_sigmoid(lora), g2_ref[...])
    kk = k * kk_ref[...]
    seg = seg_ref[...]
    kk_norm = jnp.sqrt(_dot_x_sel(kk * kk, seg))
    kk = kk / jnp.maximum(kk_norm, 1e-12)
    k = k * (1.0 + (a - 1.0) * ka_ref[...])
    bonus = _dot_x_sel(r * k * rk_ref[...], seg) * v
    rw_ref[0, :, 0 * D_MIX:1 * D_MIX] = r
    rw_ref[0, :, 1 * D_MIX:2 * D_MIX] = lw
    rw_ref[0, :, 2 * D_MIX:3 * D_MIX] = k
    rw_ref[0, :, 3 * D_MIX:4 * D_MIX] = v
    rw_ref[0, :, 4 * D_MIX:5 * D_MIX] = kk
    rw_ref[0, :, 5 * D_MIX:6 * D_MIX] = kk * a
    rw_ref[0, :, 6 * D_MIX:7 * D_MIX] = g
    rw_ref[0, :, 7 * D_MIX:8 * D_MIX] = bonus

    z = _gelu_tanh(p[:, OFF_B:OFF_C])
    su = z[:, :D_MIX]
    sv = z[:, D_MIX:]
    mu_v = jnp.mean(sv, axis=-1, keepdims=True)
    var_v = jnp.mean(jnp.square(sv - mu_v), axis=-1, keepdims=True)
    sv = (sv - mu_v) * lax.rsqrt(var_v + LN_EPS) * lnw_ref[...] + lnb_ref[...]
    row = lax.broadcasted_iota(jnp.int32, (CHUNK, CHUNK), 0)
    col = lax.broadcasted_iota(jnp.int32, (CHUNK, CHUNK), 1)
    causal = row >= col
    wsm = [jnp.where(causal, ws_ref[gi], 0.0).astype(BF16) for gi in range(G_B)]
    sv_b = sv.astype(BF16)
    for n in range(tt // CHUNK):
        rows = slice(n * CHUNK, (n + 1) * CHUNK)
        mixed = jnp.concatenate(
            [jnp.dot(wsm[gi], sv_b[rows, gi * GROUP_B:(gi + 1) * GROUP_B], preferred_element_type=F32)
             for gi in range(G_B)], axis=1) + sgb_ref[...]
        fb_ref[0, rows, :] = (su[rows, :] * mixed).astype(BF16)

    pc = p[:, OFF_C:OFF_D]
    bg = pc[:, :D_MIX]
    zc = pc[:, D_MIX:2 * D_MIX] * pc[:, 2 * D_MIX:]
    z_s[8:8 + tt, :] = zc
    y = (cw_ref[0:1, :] * z_s[6:6 + tt, :] + cw_ref[1:2, :] * z_s[7:7 + tt, :] + cw_ref[2:3, :] * zc)
    z_s[0:8, :] = z_s[tt:tt + 8, :]
    fc_ref[0] = (bg * y).astype(BF16)

    u5_ref[0] = p[:, OFF_D:OFF_G]


def _front_call(x, mod6, ng, win_s, rp, sg, cw, vmix, v_first_src, tt):
    bsz, seq, d = x.shape
    has_vmix = vmix is not None
    full = lambda a: pl.BlockSpec(a.shape, lambda b, t: (0,) * a.ndim)
    ins = [x, mod6, ng, win_s] + list(rp) + list(sg) + [cw]
    in_specs = [
        pl.BlockSpec((1, tt, d), lambda b, t: (b, t, 0)),
        pl.BlockSpec((1, 6, d), lambda b, t: (b, 0, 0)),
    ] + [full(a) for a in ins[2:]]
    if has_vmix:
        ins += [v_first_src] + list(vmix)
        in_specs += [pl.BlockSpec((1, tt, D_MIX), lambda b, t: (b, t, 3))] + [full(a) for a in vmix]
    out_shape = (
        jax.ShapeDtypeStruct((bsz, seq, RW_COLS), F32),
        jax.ShapeDtypeStruct((bsz, seq, D_MIX), BF16),
        jax.ShapeDtypeStruct((bsz, seq, D_MIX), BF16),
        jax.ShapeDtypeStruct((bsz, seq, D_MIX), F32),
    )
    out_specs = (
        pl.BlockSpec((1, tt, RW_COLS), lambda b, t: (b, t, 0)),
        pl.BlockSpec((1, tt, D_MIX), lambda b, t: (b, t, 0)),
        pl.BlockSpec((1, tt, D_MIX), lambda b, t: (b, t, 0)),
        pl.BlockSpec((1, tt, D_MIX), lambda b, t: (b, t, 0)),
    )
    return pl.pallas_call(
        functools.partial(_front_kernel, has_vmix, tt),
        grid=(bsz, seq // tt),
        in_specs=in_specs,
        out_specs=out_specs,
        out_shape=out_shape,
        scratch_shapes=[pltpu.VMEM((tt + 8, A_COLS), F32), pltpu.VMEM((tt + 8, D_MIX), F32)],
        compiler_params=_cparams(("parallel", "arbitrary")),
        name="front",
    )(*ins)


def _tri_inverse(a_strict, row, col):
    eye = jnp.where(row == col, 1.0, 0.0)
    same = lambda s: jnp.right_shift(row, s) == jnp.right_shift(col, s)
    inv = eye + jnp.where(same(1), a_strict, 0.0)
    for s in range(2, int(math.log2(WKV_CHUNK)) + 1):
        off = jnp.where(same(s), jnp.where(same(s - 1), 0.0, a_strict), 0.0)
        inv = inv + _dot(_dot(inv, off), inv)
    return inv


def _wkv_kernel(tt, rw_ref, lnw_ref, lnb_ref, o_ref, s_ref):
    n = WKV_CHUNK

    @pl.when(pl.program_id(1) == 0)
    def _():
        s_ref[...] = jnp.zeros(s_ref.shape, F32)

    row = lax.broadcasted_iota(jnp.int32, (n, n), 0)
    col = lax.broadcasted_iota(jnp.int32, (n, n), 1)
    ltri = jnp.where(row >= col, 1.0, 0.0).astype(BF16)
    strict = row > col
    incl = row >= col

    def chunk(ci, carry):
        rows = pl.ds(pl.multiple_of(ci * n, n), n)
        r = rw_ref[0, rows, 0 * D_MIX:1 * D_MIX]
        lw = rw_ref[0, rows, 1 * D_MIX:2 * D_MIX]
        k = rw_ref[0, rows, 2 * D_MIX:3 * D_MIX]
        v = rw_ref[0, rows, 3 * D_MIX:4 * D_MIX]
        kk = rw_ref[0, rows, 4 * D_MIX:5 * D_MIX]
        b = rw_ref[0, rows, 5 * D_MIX:6 * D_MIX]
        g = rw_ref[0, rows, 6 * D_MIX:7 * D_MIX]
        bonus = rw_ref[0, rows, 7 * D_MIX:8 * D_MIX]

        e = _dot_sel(ltri, lw)
        eg = jnp.exp(e)
        ig = jnp.exp(-e)
        a_t = -kk * jnp.exp(e - lw)
        r_t = r * eg
        b_t = b * ig
        k_t = k * ig
        g_end = eg[n - 1:n, :]
        b_h = b_t * g_end
        k_h = k_t * g_end

        outs = []
        for hd in range(H_A):
            cs = slice(hd * HEAD_A, (hd + 1) * HEAD_A)
            ar = jnp.concatenate([a_t[:, cs], r_t[:, cs]], axis=0)
            bk = jnp.concatenate([b_t[:, cs], k_t[:, cs]], axis=0)
            gm = _dot_nt(ar, bk)
            a_ab = jnp.where(strict, gm[:n, :n], 0.0)
            a_ak = jnp.where(strict, gm[:n, n:], 0.0)
            a_rb = jnp.where(incl, gm[n:, :n], 0.0)
            a_rk = jnp.where(incl, gm[n:, n:], 0.0)
            inv = _tri_inverse(a_ab, row, col)
            s0 = s_ref[hd]
            vh = v[:, cs]
            u = _dot(inv, _dot_nt(a_t[:, cs], s0) + _dot(a_ak, vh))
            o = _dot_nt(r_t[:, cs], s0) + _dot(a_rb, u) + _dot(a_rk, vh)
            s_ref[hd] = s0 * g_end[:, cs] + _dot_tn(u, b_h[:, cs]) + _dot_tn(vh, k_h[:, cs])
            mu = jnp.mean(o, axis=-1, keepdims=True)
            var = jnp.mean(jnp.square(o - mu), axis=-1, keepdims=True)
            outs.append((o - mu) * lax.rsqrt(var + LNX_EPS))
        on = jnp.concatenate(outs, axis=1) * lnw_ref[...] + lnb_ref[...]
        o_ref[0, rows, :] = ((on + bonus) * g).astype(BF16)
        return carry

    lax.fori_loop(0, tt // n, chunk, 0)


def _wkv_call(rw, lnw, lnb, tt):
    bsz, seq, _ = rw.shape
    return pl.pallas_call(
        functools.partial(_wkv_kernel, tt),
        grid=(bsz, seq // tt),
        in_specs=[
            pl.BlockSpec((1, tt, RW_COLS), lambda b, t: (b, t, 0)),
            pl.BlockSpec((1, D_MIX), lambda b, t: (0, 0)),
            pl.BlockSpec((1, D_MIX), lambda b, t: (0, 0)),
        ],
        out_specs=pl.BlockSpec((1, tt, D_MIX), lambda b, t: (b, t, 0)),
        out_shape=jax.ShapeDtypeStruct((bsz, seq, D_MIX), BF16),
        scratch_shapes=[pltpu.VMEM((H_A, HEAD_A, HEAD_A), F32)],
        compiler_params=_cparams(("parallel", "arbitrary")),
        name="wkv7",
    )(rw, lnw, lnb)


def _s5_param_kernel(are_ref, aim_ref, ldt_ref, bre_ref, bim_ref, abre_ref, abim_ref, bbre_ref, bbim_ref):
    lam_re = jnp.minimum(are_ref[...], -1e-4)
    lam_im = aim_ref[...]
    dt = jnp.exp(ldt_ref[...])
    mag = jnp.exp(lam_re * dt)
    ab_re = mag * jnp.cos(lam_im * dt)
    ab_im = mag * jnp.sin(lam_im * dt)
    den = lam_re * lam_re + lam_im * lam_im
    q_re = ((ab_re - 1.0) * lam_re + ab_im * lam_im) / den
    q_im = (ab_im * lam_re - (ab_re - 1.0) * lam_im) / den
    abre_ref[...] = ab_re
    abim_ref[...] = ab_im
    b_re = bre_ref[...]
    b_im = bim_ref[...]
    bbre_ref[...] = q_re * b_re - q_im * b_im
    bbim_ref[...] = q_re * b_im + q_im * b_re


def _s5_param_call(a_re, a_im, log_dt, b_re, b_im):
    gn = G_D * N_STATE
    col = lambda a: a.reshape(gn, 1)
    ldt = jnp.repeat(log_dt, N_STATE).reshape(gn, 1)
    outs = pl.pallas_call(
        _s5_param_kernel,
        out_shape=(jax.ShapeDtypeStruct((gn, 1), F32), jax.ShapeDtypeStruct((gn, 1), F32),
                   jax.ShapeDtypeStruct((gn, GROUP_D), F32), jax.ShapeDtypeStruct((gn, GROUP_D), F32)),
        name="s5_params",
    )(col(a_re), col(a_im), ldt, b_re.reshape(gn, GROUP_D), b_im.reshape(gn, GROUP_D))
    return outs


def _s5_kernel(tt, u_ref, perm_ref, bblk_ref, cre_ref, cim_ref, abre_ref, abim_ref, d_ref, o_ref,
               bu_s, x_s, carry_s):
    half = S5_LANES // 2
    steps = tt // S5_SUBSEQ

    @pl.when(pl.program_id(1) == 0)
    def _():
        carry_s[...] = jnp.zeros(carry_s.shape, F32)

    perm = perm_ref[...]
    u_p = _dot_sel(perm, u_ref[0])
    bu_s[...] = jnp.dot(u_p.astype(BF16), bblk_ref[...], preferred_element_type=F32)

    a_re = jnp.broadcast_to(abre_ref[...], (S5_SUBSEQ, half))
    a_im = jnp.broadcast_to(abim_ref[...], (S5_SUBSEQ, half))

    def step(i, st):
        s_re, s_im = st
        rows = pl.ds(pl.multiple_of(i * S5_SUBSEQ, S5_SUBSEQ), S5_SUBSEQ)
        n_re = a_re * s_re - a_im * s_im + bu_s[rows, 0:half]
        n_im = a_re * s_im + a_im * s_re + bu_s[rows, half:]
        return n_re, n_im

    zero = jnp.zeros((S5_SUBSEQ, half), F32)
    e_re, e_im = lax.fori_loop(0, steps, step, (zero, zero))

    p_re, p_im = abre_ref[...], abim_ref[...]
    for _ in range(int(math.log2(steps))):
        p_re, p_im = p_re * p_re - p_im * p_im, 2.0 * p_re * p_im

    c_re, c_im = carry_s[0:1, :], carry_s[1:2, :]
    in_re, in_im = [], []
    for j in range(S5_SUBSEQ):
        in_re.append(c_re)
        in_im.append(c_im)
        c_re, c_im = (p_re * c_re - p_im * c_im + e_re[j:j + 1, :],
                      p_re * c_im + p_im * c_re + e_im[j:j + 1, :])
    carry_s[0:1, :] = c_re
    carry_s[1:2, :] = c_im
    i_re = jnp.concatenate(in_re, axis=0)
    i_im = jnp.concatenate(in_im, axis=0)

    def step2(i, st):
        n_re, n_im = step(i, st)
        rows = pl.ds(pl.multiple_of(i * S5_SUBSEQ, S5_SUBSEQ), S5_SUBSEQ)
        x_s[rows, 0:half] = n_re
        x_s[rows, half:] = n_im
        return n_re, n_im

    lax.fori_loop(0, steps, step2, (i_re, i_im))

    y = (jnp.dot(x_s[:, 0:half].astype(BF16), cre_ref[...], preferred_element_type=F32)
         - jnp.dot(x_s[:, half:].astype(BF16), cim_ref[...], preferred_element_type=F32))
    f_p = _gelu_tanh(y + d_ref[...] * u_p).astype(BF16)
    o_ref[0] = lax.dot_general(perm, f_p, (((0,), (0,)), ((), ())),
                               preferred_element_type=F32).astype(BF16)


def _s5_call(u5, perm, bblk, cre, cim, abre, abim, dvec, tt):
    bsz, seq, _ = u5.shape
    full = lambda a: pl.BlockSpec(a.shape, lambda b, t: (0,) * a.ndim)
    consts = [perm, bblk, cre, cim, abre, abim, dvec]
    return pl.pallas_call(
        functools.partial(_s5_kernel, tt),
        grid=(bsz, seq // tt),
        in_specs=[pl.BlockSpec((1, tt, D_MIX), lambda b, t: (b, t, 0))] + [full(a) for a in consts],
        out_specs=pl.BlockSpec((1, tt, D_MIX), lambda b, t: (b, t, 0)),
        out_shape=jax.ShapeDtypeStruct((bsz, seq, D_MIX), BF16),
        scratch_shapes=[pltpu.VMEM((tt, S5_LANES), F32), pltpu.VMEM((tt, S5_LANES), F32),
                        pltpu.VMEM((8, S5_LANES // 2), F32)],
        compiler_params=_cparams(("parallel", "arbitrary")),
        name="s5_scan",
    )(u5, *consts)


def _merge_kernel(x_ref, mod_ref, ng_ref, fa_ref, fb_ref, fc_ref, fd_ref, wg_ref, wa_ref, wb_ref, wc_ref,
                  glu_ref, wo_ref, o_ref):
    d = D_MODEL
    mod = mod_ref[0]
    x = x_ref[0]
    hb = _rms_mod(x, ng_ref[...], mod[1:2, :], mod[0:1, :]).astype(BF16)

    def gate(i):
        return _sigmoid(jnp.dot(hb, wg_ref[:, i * d:(i + 1) * d], preferred_element_type=F32))

    merged = gate(0) * jnp.dot(fa_ref[0], wa_ref[...], preferred_element_type=F32)
    merged += gate(1) * jnp.dot(fb_ref[0], wb_ref[...], preferred_element_type=F32)
    merged += gate(2) * jnp.dot(fc_ref[0], wc_ref[...], preferred_element_type=F32)
    hd = jnp.dot(fd_ref[0], glu_ref[...], preferred_element_type=F32)
    merged += gate(3) * (hd[:, :d] * _sigmoid(hd[:, d:]))
    o_ref[0] = x + mod[2:3, :] * jnp.dot(merged.astype(BF16), wo_ref[...], preferred_element_type=F32)


def _merge_call(x, mod6, ng, fa, fb, fc, fd, wg, wa, wb, wc, glu, wo, tm):
    bsz, seq, d = x.shape
    full = lambda a: pl.BlockSpec(a.shape, lambda b, t: (0,) * a.ndim)
    feat = pl.BlockSpec((1, tm, D_MIX), lambda b, t: (b, t, 0))
    return pl.pallas_call(
        _merge_kernel,
        grid=(bsz, seq // tm),
        in_specs=[pl.BlockSpec((1, tm, d), lambda b, t: (b, t, 0)),
                  pl.BlockSpec((1, 6, d), lambda b, t: (b, 0, 0)),
                  full(ng), feat, feat, feat, feat,
                  full(wg), full(wa), full(wb), full(wc), full(glu), full(wo)],
        out_specs=pl.BlockSpec((1, tm, d), lambda b, t: (b, t, 0)),
        out_shape=jax.ShapeDtypeStruct((bsz, seq, d), F32),
        compiler_params=_cparams(("parallel", "parallel")),
        name="merge",
    )(x, mod6, ng, fa, fb, fc, fd, wg, wa, wb, wc, glu, wo)


def _ffn_kernel(final, x_ref, mod_ref, ng_ref, w1_ref, w2_ref, fg_ref, o_ref):
    mod = mod_ref[0]
    x = x_ref[0]
    hb = _rms_mod(x, ng_ref[...], mod[4:5, :], mod[3:4, :]).astype(BF16)
    acc = jnp.zeros(x.shape, F32)
    step = D_MODEL
    for j in range(D_FF // step):
        a = jnp.dot(hb, w1_ref[:, j * step:(j + 1) * step], preferred_element_type=F32)
        a = jnp.square(jnp.maximum(a, 0.0))
        acc += jnp.dot(a.astype(BF16), w2_ref[j * step:(j + 1) * step, :], preferred_element_type=F32)
    y = x + mod[5:6, :] * acc
    if final:
        ms = jnp.mean(y * y, axis=-1, keepdims=True)
        y = y * lax.rsqrt(ms + EPS) * fg_ref[...]
    o_ref[0] = y


def _ffn_call(x, mod6, ng, w1, w2, fg, final, tm):
    bsz, seq, d = x.shape
    full = lambda a: pl.BlockSpec(a.shape, lambda b, t: (0,) * a.ndim)
    return pl.pallas_call(
        functools.partial(_ffn_kernel, final),
        grid=(bsz, seq // tm),
        in_specs=[pl.BlockSpec((1, tm, d), lambda b, t: (b, t, 0)),
                  pl.BlockSpec((1, 6, d), lambda b, t: (b, 0, 0)),
                  full(ng), full(w1), full(w2), full(fg)],
        out_specs=pl.BlockSpec((1, tm, d), lambda b, t: (b, t, 0)),
        out_shape=jax.ShapeDtypeStruct((bsz, seq, d), F32),
        compiler_params=_cparams(("parallel", "parallel")),
        name="ffn",
    )(x, mod6, ng, w1, w2, fg)


def _row(a):
    return a.reshape(1, -1).astype(F32)


def _pad_rows(w, start, total):
    return jnp.zeros((total, w.shape[1]), w.dtype).at[start:start + w.shape[0]].set(w)


def _tile(seq, want):
    return want if seq % want == 0 else seq


def kernel(x, c, ada_w, ada_b, norm_mix_g, w_in, rwkv_mu, rwkv_w0, rwkv_w2, rwkv_a0, rwkv_a2, rwkv_g2,
           rwkv_v0, rwkv_v1, rwkv_v2, rwkv_kk, rwkv_ka, rwkv_rk, rwkv_lnx_w, rwkv_lnx_b, rwkv_out,
           sg_ln_w, sg_ln_b, sg_ws, sg_bs, sg_out, conv_w, conv_out, s5_a_re, s5_a_im, s5_b_re, s5_b_im,
           s5_c_re, s5_c_im, s5_d, s5_log_dt, s5_glu_w, w_o, norm_ffn_g, ffn_w1, ffn_w2, final_g):
    in_dtype = x.dtype
    bsz, seq, d = x.shape
    depth = ada_w.shape[0]
    x = x.astype(F32)

    tt_front = _tile(seq, 512)
    tt_wkv = _tile(seq, 512)
    tt_s5 = _tile(seq, 256)
    tm = _tile(seq, 512)

    c_rows = 16
    c_pad = jnp.zeros((c_rows, d), F32).at[:bsz].set(c.astype(F32))
    mod_all = _ada_call(c_pad, ada_w.astype(F32), ada_b.astype(F32))[:, :bsz]

    head_id = jnp.arange(D_MIX) // HEAD_A
    seg = (head_id[:, None] == head_id[None, :]).astype(BF16)
    steps = tt_s5 // S5_SUBSEQ
    dst = jnp.arange(tt_s5)
    src = (dst % S5_SUBSEQ) * steps + dst // S5_SUBSEQ
    perm = (src[:, None] == jnp.arange(tt_s5)[None, :]).astype(BF16)
    eye_g = jnp.eye(G_D, dtype=F32)

    v_first_src = None
    for l in range(depth):
        mod6 = mod_all[l].reshape(bsz, 6, d)
        win_s = w_in[l, :, :OFF_G].astype(BF16)
        rp = [
            _row(rwkv_mu[l]), _row(rwkv_w0[l]),
            _pad_rows(rwkv_w2[l], 0, LORA_COLS).astype(BF16), _row(rwkv_a0[l]),
            _pad_rows(rwkv_a2[l], LORA_W, LORA_COLS).astype(BF16),
            _pad_rows(rwkv_g2[l], LORA_W + LORA_A, LORA_COLS).astype(BF16),
            _row(rwkv_kk[l]), _row(rwkv_ka[l]), _row(rwkv_rk[l]), seg,
        ]
        sgb_full = jnp.repeat(sg_bs[l].T, GROUP_B, axis=1).astype(F32)
        sg = [_row(sg_ln_w[l]), _row(sg_ln_b[l]), sg_ws[l].astype(F32), sgb_full]
        vmix = None
        if l > 0:
            vmix = [_row(rwkv_v0[l - 1]), rwkv_v1[l - 1].astype(BF16), rwkv_v2[l - 1].astype(BF16)]
        rw, fb, fc, u5 = _front_call(x, mod6, _row(norm_mix_g[l]), win_s, rp, sg, conv_w[l].astype(F32),
                                     vmix, v_first_src, tt_front)
        if l == 0:
            v_first_src = rw

        fa = _wkv_call(rw, _row(rwkv_lnx_w[l]), _row(rwkv_lnx_b[l]), tt_wkv)

        abre, abim, bbre, bbim = _s5_param_call(s5_a_re[l].astype(F32), s5_a_im[l].astype(F32),
                                                s5_log_dt[l].astype(F32), s5_b_re[l].astype(F32),
                                                s5_b_im[l].astype(F32))

        def in_blk(bb):
            t = bb.reshape(G_D, N_STATE, GROUP_D).transpose(0, 2, 1)
            return jnp.einsum('gcn,gh->gchn', t, eye_g).reshape(D_MIX, G_D * N_STATE)

        def out_blk(cc):
            t = cc.astype(F32).transpose(0, 2, 1)
            return jnp.einsum('gnc,gh->gnhc', t, eye_g).reshape(G_D * N_STATE, D_MIX).astype(BF16)

        bblk = jnp.concatenate([in_blk(bbre), in_blk(bbim)], axis=1).astype(BF16)
        fd = _s5_call(u5, perm, bblk, out_blk(s5_c_re[l]), out_blk(s5_c_im[l]),
                      abre.reshape(1, -1), abim.reshape(1, -1), _row(s5_d[l]), tt_s5)

        x = _merge_call(x, mod6, _row(norm_mix_g[l]), fa, fb, fc, fd,
                        w_in[l, :, OFF_G:].astype(BF16), rwkv_out[l].astype(BF16), sg_out[l].astype(BF16),
                        conv_out[l].astype(BF16), s5_glu_w[l].astype(BF16), w_o[l].astype(BF16), tm)
        x = _ffn_call(x, mod6, _row(norm_ffn_g[l]), ffn_w1[l].astype(BF16), ffn_w2[l].astype(BF16),
                      _row(final_g), l == depth - 1, tm)
    return x.astype(in_dtype)
```

```python
import functools
import math

import jax
import jax.numpy as jnp
from jax import lax
from jax.experimental import pallas as pl
from jax.experimental.pallas import tpu as pltpu

F32 = jnp.float32
BF16 = jnp.bfloat16

D_MODEL = 1024
N_BRANCH = 4
D_MIX = D_MODEL // N_BRANCH
HEAD_A = 64
H_A = D_MIX // HEAD_A
LORA_W = 32
LORA_A = 32
LORA_G = 64
LNX_EPS = 64e-5
CHUNK = 128
GROUP_B = 64
G_B = D_MIX // GROUP_B
CONV_K = 3
GROUP_D = 16
G_D = D_MIX // GROUP_D
N_STATE = 64
D_FF = 4 * D_MODEL
EPS = 1e-6
LN_EPS = 1e-5

A_COLS = 3 * D_MIX + LORA_W + LORA_A + LORA_G
B_COLS = 2 * D_MIX
C_COLS = 3 * D_MIX
D_COLS = D_MIX
OFF_B = A_COLS
OFF_C = OFF_B + B_COLS
OFF_D = OFF_C + C_COLS
OFF_G = OFF_D + D_COLS
LORA_COLS = LORA_W + LORA_A + LORA_G

WKV_CHUNK = 64
S5_LANES = 2 * G_D * N_STATE
S5_SUBSEQ = 8
RW_COLS = 8 * D_MIX

VMEM_LIMIT = 56 * 1024 * 1024


def _cparams(sem):
    return pltpu.CompilerParams(dimension_semantics=sem, vmem_limit_bytes=VMEM_LIMIT)


def _dot(a, b):
    return jnp.dot(a.astype(BF16), b.astype(BF16), preferred_element_type=F32)


def _dot_nt(a, b):
    return lax.dot_general(a.astype(BF16), b.astype(BF16), (((1,), (1,)), ((), ())),
                           preferred_element_type=F32)


def _dot_tn(a, b):
    return lax.dot_general(a.astype(BF16), b.astype(BF16), (((0,), (0,)), ((), ())),
                           preferred_element_type=F32)


def _split3(x):
    hi = x.astype(BF16)
    r1 = x - hi.astype(F32)
    mid = r1.astype(BF16)
    lo = (r1 - mid.astype(F32)).astype(BF16)
    return hi, mid, lo


def _dot_sel(sel, x):
    hi, mid, lo = _split3(x)
    return (jnp.dot(sel, hi, preferred_element_type=F32) + jnp.dot(sel, mid, preferred_element_type=F32)
            + jnp.dot(sel, lo, preferred_element_type=F32))


def _dot_x_sel(x, sel):
    hi, mid, lo = _split3(x)
    return (jnp.dot(hi, sel, preferred_element_type=F32) + jnp.dot(mid, sel, preferred_element_type=F32)
            + jnp.dot(lo, sel, preferred_element_type=F32))


def _dot3(a, b):
    ah, am, _ = _split3(a)
    bh, bm, _ = _split3(b)
    return (jnp.dot(ah, bh, preferred_element_type=F32) + jnp.dot(ah, bm, preferred_element_type=F32)
            + jnp.dot(am, bh, preferred_element_type=F32))


def _sigmoid(x):
    return 1.0 / (1.0 + jnp.exp(-x))


def _gelu_tanh(x):
    return 0.5 * x * (1.0 + jnp.tanh(math.sqrt(2.0 / math.pi) * (x + 0.044715 * (x * x * x))))


def _rms_mod(x, g, sc, sh):
    ms = jnp.mean(x * x, axis=-1, keepdims=True)
    return x * lax.rsqrt(ms + EPS) * g * (1.0 + sc) + sh


def _ada_kernel(c_ref, w_ref, b_ref, o_ref):
    c = c_ref[...]
    ca = c * _sigmoid(c)
    o_ref[0] = _dot3(ca, w_ref[0]) + b_ref[0]


def _ada_call(c_pad, ada_w, ada_b):
    depth, d, n = ada_w.shape
    rows = c_pad.shape[0]
    bn = 1536
    return pl.pallas_call(
        _ada_kernel,
        grid=(depth, n // bn),
        in_specs=[
            pl.BlockSpec((rows, d), lambda l, j: (0, 0)),
            pl.BlockSpec((1, d, bn), lambda l, j: (l, 0, j)),
            pl.BlockSpec((1, 1, bn), lambda l, j: (l, 0, j)),
        ],
        out_specs=pl.BlockSpec((1, rows, bn), lambda l, j: (l, 0, j)),
        out_shape=jax.ShapeDtypeStruct((depth, rows, n), F32),
        compiler_params=_cparams(("parallel", "parallel")),
        name="ada_mod",
    )(c_pad, ada_w, ada_b.reshape(depth, 1, n))


def _front_kernel(has_vmix, tt, *refs):
    if has_vmix:
        (x_ref, mod_ref, ng_ref, win_ref, mu_ref, w0_ref, w2_ref, a0_ref, a2_ref, g2_ref, kk_ref,
         ka_ref, rk_ref, seg_ref, lnw_ref, lnb_ref, ws_ref, sgb_ref, cw_ref,
         vf_ref, v0_ref, v1_ref, v2_ref,
         rw_ref, fb_ref, fc_ref, u5_ref, pa_s, z_s) = refs
    else:
        (x_ref, mod_ref, ng_ref, win_ref, mu_ref, w0_ref, w2_ref, a0_ref, a2_ref, g2_ref, kk_ref,
         ka_ref, rk_ref, seg_ref, lnw_ref, lnb_ref, ws_ref, sgb_ref, cw_ref,
         rw_ref, fb_ref, fc_ref, u5_ref, pa_s, z_s) = refs

    t_idx = pl.program_id(1)

    @pl.when(t_idx == 0)
    def _():
        pa_s[0:8, :] = jnp.zeros((8, A_COLS), F32)
        z_s[0:8, :] = jnp.zeros((8, D_MIX), F32)

    mod = mod_ref[0]
    h = _rms_mod(x_ref[0], ng_ref[...], mod[1:2, :], mod[0:1, :])
    p = jnp.dot(h.astype(BF16), win_ref[...], preferred_element_type=F32)

    pa_s[8:8 + tt, :] = p[:, :A_COLS]
    pa = p[:, :A_COLS]
    prev = pa_s[7:7 + tt, :]
    pa = pa + (prev - pa) * mu_ref[...]
    pa_s[0:8, :] = pa_s[tt:tt + 8, :]
    r = pa[:, 0:D_MIX]
    k = pa[:, D_MIX:2 * D_MIX]
    v = pa[:, 2 * D_MIX:3 * D_MIX]
    lora = pa[:, 3 * D_MIX:A_COLS]
    lw = -math.exp(-0.5) * _sigmoid(w0_ref[...] + _dot(jnp.tanh(lora), w2_ref[...]))
    if has_vmix:
        vgate = _sigmoid(v0_ref[...] + _dot(_dot(v, v1_ref[...]), v2_ref[...]))
        v = v + (vf_ref[0] - v) * vgate
    a = _sigmoid(a0_ref[...] + _dot(lora, a2_ref[...]))
    g = _dot(_sigmoid(lora), g2_ref[...])
    kk = k * kk_ref[...]
    seg = seg_ref[...]
    kk_norm = jnp.sqrt(_dot_x_sel(kk * kk, seg))
    kk = kk / jnp.maximum(kk_norm, 1e-12)
    k = k * (1.0 + (a - 1.0) * ka_ref[...])
    bonus = _dot_x_sel(r * k * rk_ref[...], seg) * v
    rw_ref[0, :, 0 * D_MIX:1 * D_MIX] = r
    rw_ref[0, :, 1 * D_MIX:2 * D_MIX] = lw
    rw_ref[0, :, 2 * D_MIX:3 * D_MIX] = k
    rw_ref[0, :, 3 * D_MIX:4 * D_MIX] = v
    rw_ref[0, :, 4 * D_MIX:5 * D_MIX] = kk
    rw_ref[0, :, 5 * D_MIX:6 * D_MIX] = kk * a
    rw_ref[0, :, 6 * D_MIX:7 * D_MIX] = g
    rw_ref[0, :, 7 * D_MIX:8 * D_MIX] = bonus

    z = _gelu_tanh(p[:, OFF_B:OFF_C])
    su = z[:, :D_MIX]
    sv = z[:, D_MIX:]
    mu_v = jnp.mean(sv, axis=-1, keepdims=True)
    var_v = jnp.mean(jnp.square(sv - mu_v), axis=-1, keepdims=True)
    sv = (sv - mu_v) * lax.rsqrt(var_v + LN_EPS) * lnw_ref[...] + lnb_ref[...]
    row = lax.broadcasted_iota(jnp.int32, (CHUNK, CHUNK), 0)
    col = lax.broadcasted_iota(jnp.int32, (CHUNK, CHUNK), 1)
    causal = row >= col
    wsm = [jnp.where(causal, ws_ref[gi], 0.0).astype(BF16) for gi in range(G_B)]
    sv_b = sv.astype(BF16)
    for n in range(tt // CHUNK):
        rows = slice(n * CHUNK, (n + 1) * CHUNK)
        mixed = jnp.concatenate(
            [jnp.dot(wsm[gi], sv_b[rows, gi * GROUP_B:(gi + 1) * GROUP_B], preferred_element_type=F32)
             for gi in range(G_B)], axis=1) + sgb_ref[...]
        fb_ref[0, rows, :] = (su[rows, :] * mixed).astype(BF16)

    pc = p[:, OFF_C:OFF_D]
    bg = pc[:, :D_MIX]
    zc = pc[:, D_MIX:2 * D_MIX] * pc[:, 2 * D_MIX:]
    z_s[8:8 + tt, :] = zc
    y = (cw_ref[0:1, :] * z_s[6:6 + tt, :] + cw_ref[1:2, :] * z_s[7:7 + tt, :] + cw_ref[2:3, :] * zc)
    z_s[0:8, :] = z_s[tt:tt + 8, :]
    fc_ref[0] = (bg * y).astype(BF16)

    u5_ref[0] = p[:, OFF_D:OFF_G]


def _front_call(x, mod6, ng, win_s, rp, sg, cw, vmix, v_first_src, tt):
    bsz, seq, d = x.shape
    has_vmix = vmix is not None
    full = lambda a: pl.BlockSpec(a.shape, lambda b, t: (0,) * a.ndim)
    ins = [x, mod6, ng, win_s] + list(rp) + list(sg) + [cw]
    in_specs = [
        pl.BlockSpec((1, tt, d), lambda b, t: (b, t, 0)),
        pl.BlockSpec((1, 6, d), lambda b, t: (b, 0, 0)),
    ] + [full(a) for a in ins[2:]]
    if has_vmix:
        ins += [v_first_src] + list(vmix)
        in_specs += [pl.BlockSpec((1, tt, D_MIX), lambda b, t: (b, t, 3))] + [full(a) for a in vmix]
    out_shape = (
        jax.ShapeDtypeStruct((bsz, seq, RW_COLS), F32),
        jax.ShapeDtypeStruct((bsz, seq, D_MIX), BF16),
        jax.ShapeDtypeStruct((bsz, seq, D_MIX), BF16),
        jax.ShapeDtypeStruct((bsz, seq, D_MIX), F32),
    )
    out_specs = (
        pl.BlockSpec((1, tt, RW_COLS), lambda b, t: (b, t, 0)),
        pl.BlockSpec((1, tt, D_MIX), lambda b, t: (b, t, 0)),
        pl.BlockSpec((1, tt, D_MIX), lambda b, t: (b, t, 0)),
        pl.BlockSpec((1, tt, D_MIX), lambda b, t: (b, t, 0)),
    )
    return pl.pallas_call(
        functools.partial(_front_kernel, has_vmix, tt),
        grid=(bsz, seq // tt),
        in_specs=in_specs,
        out_specs=out_specs,
        out_shape=out_shape,
        scratch_shapes=[pltpu.VMEM((tt + 8, A_COLS), F32), pltpu.VMEM((tt + 8, D_MIX), F32)],
        compiler_params=_cparams(("parallel", "arbitrary")),
        name="front",
    )(*ins)


def _wkv_kernel(tt, rw_ref, lnw_ref, lnb_ref, o_ref, s_ref):
    n = WKV_CHUNK
    nc = tt // n

    @pl.when(pl.program_id(1) == 0)
    def _():
        s_ref[...] = jnp.zeros(s_ref.shape, F32)

    row = lax.broadcasted_iota(jnp.int32, (n, n), 0)
    col = lax.broadcasted_iota(jnp.int32, (n, n), 1)
    ltri = jnp.where(row >= col, 1.0, 0.0).astype(BF16)
    strict = row > col
    incl = row >= col
    eye = jnp.where(row == col, 1.0, 0.0)
    same = lambda s: jnp.right_shift(row, s) == jnp.right_shift(col, s)

    items = [(ci, hd) for ci in range(nc) for hd in range(H_A)]
    head = lambda x, hd: x[:, hd * HEAD_A:(hd + 1) * HEAD_A]

    a_t, r_t, b_t, k_t, b_h, k_h, vv, g_end = [], [], [], [], [], [], [], []
    for ci in range(nc):
        rows = slice(ci * n, (ci + 1) * n)
        r = rw_ref[0, rows, 0 * D_MIX:1 * D_MIX]
        lw = rw_ref[0, rows, 1 * D_MIX:2 * D_MIX]
        k = rw_ref[0, rows, 2 * D_MIX:3 * D_MIX]
        kk = rw_ref[0, rows, 4 * D_MIX:5 * D_MIX]
        b = rw_ref[0, rows, 5 * D_MIX:6 * D_MIX]
        e = _dot_sel(ltri, lw)
        eg = jnp.exp(e)
        ig = jnp.exp(-e)
        ge = eg[n - 1:n, :]
        a_t.append((-kk * jnp.exp(e - lw)).astype(BF16))
        r_t.append(r * eg)
        b_t.append((b * ig).astype(BF16))
        k_t.append((k * ig).astype(BF16))
        b_h.append((b * ig * ge).astype(BF16))
        k_h.append((k * ig * ge).astype(BF16))
        vv.append(rw_ref[0, rows, 3 * D_MIX:4 * D_MIX].astype(BF16))
        g_end.append(ge)

    gm = [_dot_nt(jnp.concatenate([head(a_t[ci], hd), head(r_t[ci], hd).astype(BF16)], axis=0),
                  jnp.concatenate([head(b_t[ci], hd), head(k_t[ci], hd)], axis=0)) for ci, hd in items]
    a_ab = [jnp.where(strict, m[:n, :n], 0.0) for m in gm]
    a_ak = [jnp.where(strict, m[:n, n:], 0.0) for m in gm]
    a_rb = [jnp.where(incl, m[n:, :n], 0.0) for m in gm]
    a_rk = [jnp.where(incl, m[n:, n:], 0.0) for m in gm]
    inv = [eye + jnp.where(same(1), a, 0.0) for a in a_ab]
    for s in range(2, int(math.log2(n)) + 1):
        off_mask = same(s) & jnp.logical_not(same(s - 1))
        tmp = [_dot(t, jnp.where(off_mask, a, 0.0)) for t, a in zip(inv, a_ab)]
        inv = [t + _dot(x, t) for t, x in zip(inv, tmp)]
    akv = [_dot(a_ak[i], head(vv[ci], hd)) for i, (ci, hd) in enumerate(items)]
    pq = [_dot(inv[i], jnp.concatenate([head(a_t[ci], hd), akv[i].astype(BF16)], axis=1))
          for i, (ci, hd) in enumerate(items)]
    mn = [_dot_tn(pq[i], head(b_h[ci], hd)) for i, (ci, hd) in enumerate(items)]
    kv = [_dot_tn(head(vv[ci], hd), head(k_h[ci], hd)) for ci, hd in items]
    rq = [_dot(a_rb[i], pq[i]) for i in range(len(items))]
    ark = [_dot(a_rk[i], head(vv[ci], hd)) for i, (ci, hd) in enumerate(items)]

    state = [s_ref[hd] for hd in range(H_A)]
    for ci in range(nc):
        outs = []
        for hd in range(H_A):
            i = ci * H_A + hd
            s0 = state[hd]
            ro = head(r_t[ci], hd) + rq[i][:, :n]
            o = _dot_nt(ro, s0) + rq[i][:, n:] + ark[i]
            state[hd] = s0 * head(g_end[ci], hd) + _dot(s0, mn[i][:n, :]) + (mn[i][n:, :] + kv[i])
            mu = jnp.mean(o, axis=-1, keepdims=True)
            var = jnp.mean(jnp.square(o - mu), axis=-1, keepdims=True)
            outs.append((o - mu) * lax.rsqrt(var + LNX_EPS))
        rows = slice(ci * n, (ci + 1) * n)
        on = jnp.concatenate(outs, axis=1) * lnw_ref[...] + lnb_ref[...]
        g = rw_ref[0, rows, 6 * D_MIX:7 * D_MIX]
        bonus = rw_ref[0, rows, 7 * D_MIX:8 * D_MIX]
        o_ref[0, rows, :] = ((on + bonus) * g).astype(BF16)
    for hd in range(H_A):
        s_ref[hd] = state[hd]


def _wkv_call(rw, lnw, lnb, tt):
    bsz, seq, _ = rw.shape
    return pl.pallas_call(
        functools.partial(_wkv_kernel, tt),
        grid=(bsz, seq // tt),
        in_specs=[
            pl.BlockSpec((1, tt, RW_COLS), lambda b, t: (b, t, 0)),
            pl.BlockSpec((1, D_MIX), lambda b, t: (0, 0)),
            pl.BlockSpec((1, D_MIX), lambda b, t: (0, 0)),
        ],
        out_specs=pl.BlockSpec((1, tt, D_MIX), lambda b, t: (b, t, 0)),
        out_shape=jax.ShapeDtypeStruct((bsz, seq, D_MIX), BF16),
        scratch_shapes=[pltpu.VMEM((H_A, HEAD_A, HEAD_A), F32)],
        compiler_params=_cparams(("parallel", "arbitrary")),
        name="wkv7",
    )(rw, lnw, lnb)


def _s5_param_kernel(are_ref, aim_ref, ldt_ref, bre_ref, bim_ref, abre_ref, abim_ref, bbre_ref, bbim_ref):
    lam_re = jnp.minimum(are_ref[...], -1e-4)
    lam_im = aim_ref[...]
    dt = jnp.exp(ldt_ref[...])
    mag = jnp.exp(lam_re * dt)
    ab_re = mag * jnp.cos(lam_im * dt)
    ab_im = mag * jnp.sin(lam_im * dt)
    den = lam_re * lam_re + lam_im * lam_im
    q_re = ((ab_re - 1.0) * lam_re + ab_im * lam_im) / den
    q_im = (ab_im * lam_re - (ab_re - 1.0) * lam_im) / den
    abre_ref[...] = ab_re
    abim_ref[...] = ab_im
    b_re = bre_ref[...]
    b_im = bim_ref[...]
    bbre_ref[...] = q_re * b_re - q_im * b_im
    bbim_ref[...] = q_re * b_im + q_im * b_re


def _s5_param_call(a_re, a_im, log_dt, b_re, b_im):
    gn = G_D * N_STATE
    col = lambda a: a.reshape(gn, 1)
    ldt = jnp.repeat(log_dt, N_STATE).reshape(gn, 1)
    outs = pl.pallas_call(
        _s5_param_kernel,
        out_shape=(jax.ShapeDtypeStruct((gn, 1), F32), jax.ShapeDtypeStruct((gn, 1), F32),
                   jax.ShapeDtypeStruct((gn, GROUP_D), F32), jax.ShapeDtypeStruct((gn, GROUP_D), F32)),
        name="s5_params",
    )(col(a_re), col(a_im), ldt, b_re.reshape(gn, GROUP_D), b_im.reshape(gn, GROUP_D))
    return outs


def _s5_kernel(tt, u_ref, perm_ref, bblk_ref, cre_ref, cim_ref, abre_ref, abim_ref, d_ref, o_ref,
               bu_s, x_s, carry_s):
    half = S5_LANES // 2
    steps = tt // S5_SUBSEQ

    @pl.when(pl.program_id(1) == 0)
    def _():
        carry_s[...] = jnp.zeros(carry_s.shape, F32)

    perm = perm_ref[...]
    u_p = _dot_sel(perm, u_ref[0])
    bu_s[...] = jnp.dot(u_p.astype(BF16), bblk_ref[...], preferred_element_type=F32)

    a_re = jnp.broadcast_to(abre_ref[...], (S5_SUBSEQ, half))
    a_im = jnp.broadcast_to(abim_ref[...], (S5_SUBSEQ, half))

    def step(i, st):
        s_re, s_im = st
        rows = pl.ds(pl.multiple_of(i * S5_SUBSEQ, S5_SUBSEQ), S5_SUBSEQ)
        n_re = a_re * s_re - a_im * s_im + bu_s[rows, 0:half]
        n_im = a_re * s_im + a_im * s_re + bu_s[rows, half:]
        return n_re, n_im

    zero = jnp.zeros((S5_SUBSEQ, half), F32)
    e_re, e_im = lax.fori_loop(0, steps, step, (zero, zero))

    p_re, p_im = abre_ref[...], abim_ref[...]
    for _ in range(int(math.log2(steps))):
        p_re, p_im = p_re * p_re - p_im * p_im, 2.0 * p_re * p_im

    c_re, c_im = carry_s[0:1, :], carry_s[1:2, :]
    in_re, in_im = [], []
    for j in range(S5_SUBSEQ):
        in_re.append(c_re)
        in_im.append(c_im)
        c_re, c_im = (p_re * c_re - p_im * c_im + e_re[j:j + 1, :],
                      p_re * c_im + p_im * c_re + e_im[j:j + 1, :])
    carry_s[0:1, :] = c_re
    carry_s[1:2, :] = c_im
    i_re = jnp.concatenate(in_re, axis=0)
    i_im = jnp.concatenate(in_im, axis=0)

    def step2(i, st):
        n_re, n_im = step(i, st)
        rows = pl.ds(pl.multiple_of(i * S5_SUBSEQ, S5_SUBSEQ), S5_SUBSEQ)
        x_s[rows, 0:half] = n_re
        x_s[rows, half:] = n_im
        return n_re, n_im

    lax.fori_loop(0, steps, step2, (i_re, i_im))

    y = (jnp.dot(x_s[:, 0:half].astype(BF16), cre_ref[...], preferred_element_type=F32)
         - jnp.dot(x_s[:, half:].astype(BF16), cim_ref[...], preferred_element_type=F32))
    f_p = _gelu_tanh(y + d_ref[...] * u_p).astype(BF16)
    o_ref[0] = lax.dot_general(perm, f_p, (((0,), (0,)), ((), ())),
                               preferred_element_type=F32).astype(BF16)


def _s5_call(u5, perm, bblk, cre, cim, abre, abim, dvec, tt):
    bsz, seq, _ = u5.shape
    full = lambda a: pl.BlockSpec(a.shape, lambda b, t: (0,) * a.ndim)
    consts = [perm, bblk, cre, cim, abre, abim, dvec]
    return pl.pallas_call(
        functools.partial(_s5_kernel, tt),
        grid=(bsz, seq // tt),
        in_specs=[pl.BlockSpec((1, tt, D_MIX), lambda b, t: (b, t, 0))] + [full(a) for a in consts],
        out_specs=pl.BlockSpec((1, tt, D_MIX), lambda b, t: (b, t, 0)),
        out_shape=jax.ShapeDtypeStruct((bsz, seq, D_MIX), BF16),
        scratch_shapes=[pltpu.VMEM((tt, S5_LANES), F32), pltpu.VMEM((tt, S5_LANES), F32),
                        pltpu.VMEM((8, S5_LANES // 2), F32)],
        compiler_params=_cparams(("parallel", "arbitrary")),
        name="s5_scan",
    )(u5, *consts)


def _merge_kernel(x_ref, mod_ref, ng_ref, fa_ref, fb_ref, fc_ref, fd_ref, wg_ref, wa_ref, wb_ref, wc_ref,
                  glu_ref, wo_ref, o_ref):
    d = D_MODEL
    mod = mod_ref[0]
    x = x_ref[0]
    hb = _rms_mod(x, ng_ref[...], mod[1:2, :], mod[0:1, :]).astype(BF16)

    def gate(i):
        return _sigmoid(jnp.dot(hb, wg_ref[:, i * d:(i + 1) * d], preferred_element_type=F32))

    merged = gate(0) * jnp.dot(fa_ref[0], wa_ref[...], preferred_element_type=F32)
    merged += gate(1) * jnp.dot(fb_ref[0], wb_ref[...], preferred_element_type=F32)
    merged += gate(2) * jnp.dot(fc_ref[0], wc_ref[...], preferred_element_type=F32)
    hd = jnp.dot(fd_ref[0], glu_ref[...], preferred_element_type=F32)
    merged += gate(3) * (hd[:, :d] * _sigmoid(hd[:, d:]))
    o_ref[0] = x + mod[2:3, :] * jnp.dot(merged.astype(BF16), wo_ref[...], preferred_element_type=F32)


def _merge_call(x, mod6, ng, fa, fb, fc, fd, wg, wa, wb, wc, glu, wo, tm):
    bsz, seq, d = x.shape
    full = lambda a: pl.BlockSpec(a.shape, lambda b, t: (0,) * a.ndim)
    feat = pl.BlockSpec((1, tm, D_MIX), lambda b, t: (b, t, 0))
    return pl.pallas_call(
        _merge_kernel,
        grid=(bsz, seq // tm),
        in_specs=[pl.BlockSpec((1, tm, d), lambda b, t: (b, t, 0)),
                  pl.BlockSpec((1, 6, d), lambda b, t: (b, 0, 0)),
                  full(ng), feat, feat, feat, feat,
                  full(wg), full(wa), full(wb), full(wc), full(glu), full(wo)],
        out_specs=pl.BlockSpec((1, tm, d), lambda b, t: (b, t, 0)),
        out_shape=jax.ShapeDtypeStruct((bsz, seq, d), F32),
        compiler_params=_cparams(("parallel", "parallel")),
        name="merge",
    )(x, mod6, ng, fa, fb, fc, fd, wg, wa, wb, wc, glu, wo)


def _ffn_kernel(final, x_ref, mod_ref, ng_ref, w1_ref, w2_ref, fg_ref, o_ref):
    mod = mod_ref[0]
    x = x_ref[0]
    hb = _rms_mod(x, ng_ref[...], mod[4:5, :], mod[3:4, :]).astype(BF16)
    acc = jnp.zeros(x.shape, F32)
    step = D_MODEL
    for j in range(D_FF // step):
        a = jnp.dot(hb, w1_ref[:, j * step:(j + 1) * step], preferred_element_type=F32)
        a = jnp.square(jnp.maximum(a, 0.0))
        acc += jnp.dot(a.astype(BF16), w2_ref[j * step:(j + 1) * step, :], preferred_element_type=F32)
    y = x + mod[5:6, :] * acc
    if final:
        ms = jnp.mean(y * y, axis=-1, keepdims=True)
        y = y * lax.rsqrt(ms + EPS) * fg_ref[...]
    o_ref[0] = y


def _ffn_call(x, mod6, ng, w1, w2, fg, final, tm):
    bsz, seq, d = x.shape
    full = lambda a: pl.BlockSpec(a.shape, lambda b, t: (0,) * a.ndim)
    return pl.pallas_call(
        functools.partial(_ffn_kernel, final),
        grid=(bsz, seq // tm),
        in_specs=[pl.BlockSpec((1, tm, d), lambda b, t: (b, t, 0)),
                  pl.BlockSpec((1, 6, d), lambda b, t: (b, 0, 0)),
                  full(ng), full(w1), full(w2), full(fg)],
        out_specs=pl.BlockSpec((1, tm, d), lambda b, t: (b, t, 0)),
        out_shape=jax.ShapeDtypeStruct((bsz, seq, d), F32),
        compiler_params=_cparams(("parallel", "parallel")),
        name="ffn",
    )(x, mod6, ng, w1, w2, fg)


def _row(a):
    return a.reshape(1, -1).astype(F32)


def _pad_rows(w, start, total):
    return jnp.zeros((total, w.shape[1]), w.dtype).at[start:start + w.shape[0]].set(w)


def _tile(seq, want):
    return want if seq % want == 0 else seq


def kernel(x, c, ada_w, ada_b, norm_mix_g, w_in, rwkv_mu, rwkv_w0, rwkv_w2, rwkv_a0, rwkv_a2, rwkv_g2,
           rwkv_v0, rwkv_v1, rwkv_v2, rwkv_kk, rwkv_ka, rwkv_rk, rwkv_lnx_w, rwkv_lnx_b, rwkv_out,
           sg_ln_w, sg_ln_b, sg_ws, sg_bs, sg_out, conv_w, conv_out, s5_a_re, s5_a_im, s5_b_re, s5_b_im,
           s5_c_re, s5_c_im, s5_d, s5_log_dt, s5_glu_w, w_o, norm_ffn_g, ffn_w1, ffn_w2, final_g):
    in_dtype = x.dtype
    bsz, seq, d = x.shape
    depth = ada_w.shape[0]
    x = x.astype(F32)

    tt_front = _tile(seq, 512)
    tt_wkv = _tile(seq, 256)
    tt_s5 = _tile(seq, 256)
    tm = _tile(seq, 512)

    c_rows = 16
    c_pad = jnp.zeros((c_rows, d), F32).at[:bsz].set(c.astype(F32))
    mod_all = _ada_call(c_pad, ada_w.astype(F32), ada_b.astype(F32))[:, :bsz]

    head_id = jnp.arange(D_MIX) // HEAD_A
    seg = (head_id[:, None] == head_id[None, :]).astype(BF16)
    steps = tt_s5 // S5_SUBSEQ
    dst = jnp.arange(tt_s5)
    src = (dst % S5_SUBSEQ) * steps + dst // S5_SUBSEQ
    perm = (src[:, None] == jnp.arange(tt_s5)[None, :]).astype(BF16)
    eye_g = jnp.eye(G_D, dtype=F32)

    v_first_src = None
    for l in range(depth):
        mod6 = mod_all[l].reshape(bsz, 6, d)
        win_s = w_in[l, :, :OFF_G].astype(BF16)
        rp = [
            _row(rwkv_mu[l]), _row(rwkv_w0[l]),
            _pad_rows(rwkv_w2[l], 0, LORA_COLS).astype(BF16), _row(rwkv_a0[l]),
            _pad_rows(rwkv_a2[l], LORA_W, LORA_COLS).astype(BF16),
            _pad_rows(rwkv_g2[l], LORA_W + LORA_A, LORA_COLS).astype(BF16),
            _row(rwkv_kk[l]), _row(rwkv_ka[l]), _row(rwkv_rk[l]), seg,
        ]
        sgb_full = jnp.repeat(sg_bs[l].T, GROUP_B, axis=1).astype(F32)
        sg = [_row(sg_ln_w[l]), _row(sg_ln_b[l]), sg_ws[l].astype(F32), sgb_full]
        vmix = None
        if l > 0:
            vmix = [_row(rwkv_v0[l - 1]), rwkv_v1[l - 1].astype(BF16), rwkv_v2[l - 1].astype(BF16)]
        rw, fb, fc, u5 = _front_call(x, mod6, _row(norm_mix_g[l]), win_s, rp, sg, conv_w[l].astype(F32),
                                     vmix, v_first_src, tt_front)
        if l == 0:
            v_first_src = rw

        fa = _wkv_call(rw, _row(rwkv_lnx_w[l]), _row(rwkv_lnx_b[l]), tt_wkv)

        abre, abim, bbre, bbim = _s5_param_call(s5_a_re[l].astype(F32), s5_a_im[l].astype(F32),
                                                s5_log_dt[l].astype(F32), s5_b_re[l].astype(F32),
                                                s5_b_im[l].astype(F32))

        def in_blk(bb):
            t = bb.reshape(G_D, N_STATE, GROUP_D).transpose(0, 2, 1)
            return jnp.einsum('gcn,gh->gchn', t, eye_g).reshape(D_MIX, G_D * N_STATE)

        def out_blk(cc):
            t = cc.astype(F32).transpose(0, 2, 1)
            return jnp.einsum('gnc,gh->gnhc', t, eye_g).reshape(G_D * N_STATE, D_MIX).astype(BF16)

        bblk = jnp.concatenate([in_blk(bbre), in_blk(bbim)], axis=1).astype(BF16)
        fd = _s5_call(u5, perm, bblk, out_blk(s5_c_re[l]), out_blk(s5_c_im[l]),
                      abre.reshape(1, -1), abim.reshape(1, -1), _row(s5_d[l]), tt_s5)

        x = _merge_call(x, mod6, _row(norm_mix_g[l]), fa, fb, fc, fd,
                        w_in[l, :, OFF_G:].astype(BF16), rwkv_out[l].astype(BF16), sg_out[l].astype(BF16),
                        conv_out[l].astype(BF16), s5_glu_w[l].astype(BF16), w_o[l].astype(BF16), tm)
        x = _ffn_call(x, mod6, _row(norm_ffn_g[l]), ffn_w1[l].astype(BF16), ffn_w2[l].astype(BF16),
                      _row(final_g), l == depth - 1, tm)
    return x.astype(in_dtype)
```

```python
import functools
import math

import numpy as np
import jax
import jax.numpy as jnp
from jax import lax
from jax.experimental import pallas as pl
from jax.experimental.pallas import tpu as pltpu

F32 = jnp.float32
BF16 = jnp.bfloat16

D_MODEL = 1024
N_BRANCH = 4
D_MIX = D_MODEL // N_BRANCH
HEAD_A = 64
H_A = D_MIX // HEAD_A
LORA_W = 32
LORA_A = 32
LORA_G = 64
LNX_EPS = 64e-5
CHUNK = 128
GROUP_B = 64
G_B = D_MIX // GROUP_B
CONV_K = 3
GROUP_D = 16
G_D = D_MIX // GROUP_D
N_STATE = 64
D_FF = 4 * D_MODEL
EPS = 1e-6
LN_EPS = 1e-5

A_COLS = 3 * D_MIX + LORA_W + LORA_A + LORA_G
B_COLS = 2 * D_MIX
C_COLS = 3 * D_MIX
D_COLS = D_MIX
OFF_B = A_COLS
OFF_C = OFF_B + B_COLS
OFF_D = OFF_C + C_COLS
OFF_G = OFF_D + D_COLS
LORA_COLS = LORA_W + LORA_A + LORA_G

WKV_CHUNK = 64
S5_STATES = G_D * N_STATE
S5_LANES = 2 * S5_STATES
S5_SUBSEQ = 8
RW_COLS = 8 * D_MIX

(ROW_W0, ROW_A0, ROW_KK, ROW_KA, ROW_RK, ROW_V0, ROW_LNXW, ROW_LNXB, ROW_SGW, ROW_SGB, ROW_CW) = range(11)
ROW_S5D = ROW_CW + CONV_K
N_ROWS = 16

VMEM_LIMIT = 56 * 1024 * 1024


def _cparams(sem):
    return pltpu.CompilerParams(dimension_semantics=sem, vmem_limit_bytes=VMEM_LIMIT)


def _dot(a, b):
    return jnp.dot(a.astype(BF16), b.astype(BF16), preferred_element_type=F32)


def _dot_nt(a, b):
    return lax.dot_general(a.astype(BF16), b.astype(BF16), (((1,), (1,)), ((), ())),
                           preferred_element_type=F32)


def _dot_tn(a, b):
    return lax.dot_general(a.astype(BF16), b.astype(BF16), (((0,), (0,)), ((), ())),
                           preferred_element_type=F32)


def _split3(x):
    hi = x.astype(BF16)
    r1 = x - hi.astype(F32)
    mid = r1.astype(BF16)
    lo = (r1 - mid.astype(F32)).astype(BF16)
    return hi, mid, lo


def _dot_sel(sel, x):
    hi, mid, lo = _split3(x)
    return (jnp.dot(sel, hi, preferred_element_type=F32) + jnp.dot(sel, mid, preferred_element_type=F32)
            + jnp.dot(sel, lo, preferred_element_type=F32))


def _dot_x_sel(x, sel):
    hi, mid, lo = _split3(x)
    return (jnp.dot(hi, sel, preferred_element_type=F32) + jnp.dot(mid, sel, preferred_element_type=F32)
            + jnp.dot(lo, sel, preferred_element_type=F32))


def _dot3(a, b):
    ah, am, _ = _split3(a)
    bh, bm, _ = _split3(b)
    return (jnp.dot(ah, bh, preferred_element_type=F32) + jnp.dot(ah, bm, preferred_element_type=F32)
            + jnp.dot(am, bh, preferred_element_type=F32))


def _sigmoid(x):
    return 0.5 * jnp.tanh(0.5 * x) + 0.5


def _gelu_tanh(x):
    return 0.5 * x * (1.0 + jnp.tanh(math.sqrt(2.0 / math.pi) * (x + 0.044715 * (x * x * x))))


def _rms_mod(x, g, sc, sh):
    ms = jnp.mean(x * x, axis=-1, keepdims=True)
    return x * lax.rsqrt(ms + EPS) * g * (1.0 + sc) + sh


def _layer_spec(a, l):
    return pl.BlockSpec((1,) + a.shape[1:], lambda b, t: (l,) + (0,) * (a.ndim - 1))


def _mod_spec(mod4, l):
    return pl.BlockSpec((1, 1) + mod4.shape[2:], lambda b, t: (l, b, 0, 0))


def _const_spec(a):
    return pl.BlockSpec(a.shape, lambda b, t: (0,) * a.ndim)


def _tok_spec(tt, cols):
    return pl.BlockSpec((1, tt, cols), lambda b, t: (b, t, 0))


def _ada_kernel(c_ref, w_ref, b_ref, o_ref):
    c = c_ref[...]
    ca = c * _sigmoid(c)
    o_ref[0] = _dot3(ca, w_ref[0]) + b_ref[0]


def _ada_call(c_pad, ada_w, ada_b):
    depth, d, n = ada_w.shape
    rows = c_pad.shape[0]
    bn = 1536
    return pl.pallas_call(
        _ada_kernel,
        grid=(depth, n // bn),
        in_specs=[
            pl.BlockSpec((rows, d), lambda l, j: (0, 0)),
            pl.BlockSpec((1, d, bn), lambda l, j: (l, 0, j)),
            pl.BlockSpec((1, 1, bn), lambda l, j: (l, 0, j)),
        ],
        out_specs=pl.BlockSpec((1, rows, bn), lambda l, j: (l, 0, j)),
        out_shape=jax.ShapeDtypeStruct((depth, rows, n), F32),
        compiler_params=_cparams(("parallel", "parallel")),
        name="ada_mod",
    )(c_pad, ada_w, ada_b.reshape(depth, 1, n))


def _front_kernel(has_vmix, tt, *refs):
    if has_vmix:
        (x_ref, mod_ref, ng_ref, win_ref, mu_ref, rows_ref, lw_ref, seg_ref, ws_ref, sgb_ref,
         vf_ref, v1_ref, v2_ref,
         rw_ref, fb_ref, fc_ref, u5_ref, pa_s, z_s) = refs
    else:
        (x_ref, mod_ref, ng_ref, win_ref, mu_ref, rows_ref, lw_ref, seg_ref, ws_ref, sgb_ref,
         rw_ref, fb_ref, fc_ref, u5_ref, pa_s, z_s) = refs
    prow = lambda i: rows_ref[0, i:i + 1, :]

    @pl.when(pl.program_id(1) == 0)
    def _():
        pa_s[0:8, :] = jnp.zeros((8, A_COLS), F32)
        z_s[0:8, :] = jnp.zeros((8, D_MIX), F32)

    mod = mod_ref[0, 0]
    h = _rms_mod(x_ref[0], ng_ref[0], mod[1:2, :], mod[0:1, :])
    p = jnp.dot(h.astype(BF16), win_ref[0], preferred_element_type=F32)

    pa_s[8:8 + tt, :] = p[:, :A_COLS]
    pa = p[:, :A_COLS]
    prev = pa_s[7:7 + tt, :]
    pa = pa + (prev - pa) * mu_ref[0]
    pa_s[0:8, :] = pa_s[tt:tt + 8, :]
    r = pa[:, 0:D_MIX]
    k = pa[:, D_MIX:2 * D_MIX]
    v = pa[:, 2 * D_MIX:3 * D_MIX]
    lora = pa[:, 3 * D_MIX:A_COLS]
    lw = -math.exp(-0.5) * _sigmoid(prow(ROW_W0) + _dot(jnp.tanh(lora), lw_ref[0, :, 0:D_MIX]))
    if has_vmix:
        vgate = _sigmoid(prow(ROW_V0) + _dot(_dot(v, v1_ref[0]), v2_ref[0]))
        v = v + (vf_ref[0] - v) * vgate
    a = _sigmoid(prow(ROW_A0) + _dot(lora, lw_ref[0, :, D_MIX:2 * D_MIX]))
    g = _dot(_sigmoid(lora), lw_ref[0, :, 2 * D_MIX:])
    kk = k * prow(ROW_KK)
    seg = seg_ref[...]
    kk_norm = jnp.sqrt(_dot_x_sel(kk * kk, seg))
    kk = kk / jnp.maximum(kk_norm, 1e-12)
    k = k * (1.0 + (a - 1.0) * prow(ROW_KA))
    bonus = _dot_x_sel(r * k * prow(ROW_RK), seg) * v
    rw_ref[0, :, 0 * D_MIX:1 * D_MIX] = r
    rw_ref[0, :, 1 * D_MIX:2 * D_MIX] = lw
    rw_ref[0, :, 2 * D_MIX:3 * D_MIX] = k
    rw_ref[0, :, 3 * D_MIX:4 * D_MIX] = v
    rw_ref[0, :, 4 * D_MIX:5 * D_MIX] = kk
    rw_ref[0, :, 5 * D_MIX:6 * D_MIX] = kk * a
    rw_ref[0, :, 6 * D_MIX:7 * D_MIX] = g
    rw_ref[0, :, 7 * D_MIX:8 * D_MIX] = bonus

    z = _gelu_tanh(p[:, OFF_B:OFF_C])
    su = z[:, :D_MIX]
    sv = z[:, D_MIX:]
    mu_v = jnp.mean(sv, axis=-1, keepdims=True)
    var_v = jnp.mean(jnp.square(sv - mu_v), axis=-1, keepdims=True)
    sv = (sv - mu_v) * lax.rsqrt(var_v + LN_EPS) * prow(ROW_SGW) + prow(ROW_SGB)
    row = lax.broadcasted_iota(jnp.int32, (CHUNK, CHUNK), 0)
    col = lax.broadcasted_iota(jnp.int32, (CHUNK, CHUNK), 1)
    causal = row >= col
    wsm = [jnp.where(causal, ws_ref[0, gi], 0.0).astype(BF16) for gi in range(G_B)]
    sv_b = sv.astype(BF16)
    for n in range(tt // CHUNK):
        rows = slice(n * CHUNK, (n + 1) * CHUNK)
        mixed = jnp.concatenate(
            [jnp.dot(wsm[gi], sv_b[rows, gi * GROUP_B:(gi + 1) * GROUP_B], preferred_element_type=F32)
             for gi in range(G_B)], axis=1) + sgb_ref[0]
        fb_ref[0, rows, :] = (su[rows, :] * mixed).astype(BF16)

    pc = p[:, OFF_C:OFF_D]
    bg = pc[:, :D_MIX]
    zc = pc[:, D_MIX:2 * D_MIX] * pc[:, 2 * D_MIX:]
    z_s[8:8 + tt, :] = zc
    y = prow(ROW_CW) * z_s[6:6 + tt, :] + prow(ROW_CW + 1) * z_s[7:7 + tt, :] + prow(ROW_CW + 2) * zc
    z_s[0:8, :] = z_s[tt:tt + 8, :]
    fc_ref[0] = (bg * y).astype(BF16)

    u5_ref[0] = p[:, OFF_D:OFF_G]


def _front_call(l, x, mod4, ng, win_s, mu, rows, lora_w, seg, ws, sgb, v1, v2, v_first_src, tt):
    bsz, seq, d = x.shape
    has_vmix = l > 0
    layer = [ng, win_s, mu, rows, lora_w]
    ins = [x, mod4] + layer + [seg, ws, sgb]
    in_specs = ([_tok_spec(tt, d), _mod_spec(mod4, l)] + [_layer_spec(a, l) for a in layer]
                + [_const_spec(seg), _layer_spec(ws, l), _layer_spec(sgb, l)])
    if has_vmix:
        ins += [v_first_src, v1, v2]
        in_specs += [pl.BlockSpec((1, tt, D_MIX), lambda b, t: (b, t, 3)),
                     _layer_spec(v1, l - 1), _layer_spec(v2, l - 1)]
    out_shape = (
        jax.ShapeDtypeStruct((bsz, seq, RW_COLS), F32),
        jax.ShapeDtypeStruct((bsz, seq, D_MIX), BF16),
        jax.ShapeDtypeStruct((bsz, seq, D_MIX), BF16),
        jax.ShapeDtypeStruct((bsz, seq, D_MIX), F32),
    )
    out_specs = (_tok_spec(tt, RW_COLS), _tok_spec(tt, D_MIX), _tok_spec(tt, D_MIX), _tok_spec(tt, D_MIX))
    return pl.pallas_call(
        functools.partial(_front_kernel, has_vmix, tt),
        grid=(bsz, seq // tt),
        in_specs=in_specs,
        out_specs=out_specs,
        out_shape=out_shape,
        scratch_shapes=[pltpu.VMEM((tt + 8, A_COLS), F32), pltpu.VMEM((tt + 8, D_MIX), F32)],
        compiler_params=_cparams(("parallel", "arbitrary")),
        name="front",
    )(*ins)


def _wkv_kernel(tt, rw_ref, rows_ref, o_ref, s_ref):
    n = WKV_CHUNK
    nc = tt // n

    @pl.when(pl.program_id(1) == 0)
    def _():
        s_ref[...] = jnp.zeros(s_ref.shape, F32)

    row = lax.broadcasted_iota(jnp.int32, (n, n), 0)
    col = lax.broadcasted_iota(jnp.int32, (n, n), 1)
    ltri = jnp.where(row >= col, 1.0, 0.0).astype(BF16)
    strict = row > col
    incl = row >= col
    eye = jnp.where(row == col, 1.0, 0.0)
    same = lambda s: jnp.right_shift(row, s) == jnp.right_shift(col, s)

    items = [(ci, hd) for ci in range(nc) for hd in range(H_A)]
    head = lambda x, hd: x[:, hd * HEAD_A:(hd + 1) * HEAD_A]

    a_t, r_t, b_t, k_t, b_h, k_h, vv, g_end = [], [], [], [], [], [], [], []
    for ci in range(nc):
        rows = slice(ci * n, (ci + 1) * n)
        r = rw_ref[0, rows, 0 * D_MIX:1 * D_MIX]
        lw = rw_ref[0, rows, 1 * D_MIX:2 * D_MIX]
        k = rw_ref[0, rows, 2 * D_MIX:3 * D_MIX]
        kk = rw_ref[0, rows, 4 * D_MIX:5 * D_MIX]
        b = rw_ref[0, rows, 5 * D_MIX:6 * D_MIX]
        e = _dot_sel(ltri, lw)
        eg = jnp.exp(e)
        ig = jnp.exp(-e)
        ge = eg[n - 1:n, :]
        a_t.append((-kk * jnp.exp(e - lw)).astype(BF16))
        r_t.append(r * eg)
        b_t.append((b * ig).astype(BF16))
        k_t.append((k * ig).astype(BF16))
        b_h.append((b * ig * ge).astype(BF16))
        k_h.append((k * ig * ge).astype(BF16))
        vv.append(rw_ref[0, rows, 3 * D_MIX:4 * D_MIX].astype(BF16))
        g_end.append(ge)

    gm = [_dot_nt(jnp.concatenate([head(a_t[ci], hd), head(r_t[ci], hd).astype(BF16)], axis=0),
                  jnp.concatenate([head(b_t[ci], hd), head(k_t[ci], hd)], axis=0)) for ci, hd in items]
    a_ab = [jnp.where(strict, m[:n, :n], 0.0) for m in gm]
    a_ak = [jnp.where(strict, m[:n, n:], 0.0) for m in gm]
    a_rb = [jnp.where(incl, m[n:, :n], 0.0) for m in gm]
    a_rk = [jnp.where(incl, m[n:, n:], 0.0) for m in gm]
    inv = [eye + jnp.where(same(1), a, 0.0) for a in a_ab]
    for s in range(2, int(math.log2(n)) + 1):
        off_mask = same(s) & jnp.logical_not(same(s - 1))
        tmp = [_dot(t, jnp.where(off_mask, a, 0.0)) for t, a in zip(inv, a_ab)]
        inv = [t + _dot(x, t) for t, x in zip(inv, tmp)]
    akv = [_dot(a_ak[i], head(vv[ci], hd)) for i, (ci, hd) in enumerate(items)]
    pq = [_dot(inv[i], jnp.concatenate([head(a_t[ci], hd), akv[i].astype(BF16)], axis=1))
          for i, (ci, hd) in enumerate(items)]
    mn = [_dot_tn(pq[i], head(b_h[ci], hd)) for i, (ci, hd) in enumerate(items)]
    kv = [_dot_tn(head(vv[ci], hd), head(k_h[ci], hd)) for ci, hd in items]
    rq = [_dot(a_rb[i], pq[i]) for i in range(len(items))]
    ark = [_dot(a_rk[i], head(vv[ci], hd)) for i, (ci, hd) in enumerate(items)]

    state = [s_ref[hd] for hd in range(H_A)]
    for ci in range(nc):
        outs = []
        for hd in range(H_A):
            i = ci * H_A + hd
            s0 = state[hd]
            ro = head(r_t[ci], hd) + rq[i][:, :n]
            o = _dot_nt(ro, s0) + rq[i][:, n:] + ark[i]
            state[hd] = s0 * head(g_end[ci], hd) + _dot(s0, mn[i][:n, :]) + (mn[i][n:, :] + kv[i])
            mu = jnp.mean(o, axis=-1, keepdims=True)
            var = jnp.mean(jnp.square(o - mu), axis=-1, keepdims=True)
            outs.append((o - mu) * lax.rsqrt(var + LNX_EPS))
        rows = slice(ci * n, (ci + 1) * n)
        on = (jnp.concatenate(outs, axis=1) * rows_ref[0, ROW_LNXW:ROW_LNXW + 1, :]
              + rows_ref[0, ROW_LNXB:ROW_LNXB + 1, :])
        g = rw_ref[0, rows, 6 * D_MIX:7 * D_MIX]
        bonus = rw_ref[0, rows, 7 * D_MIX:8 * D_MIX]
        o_ref[0, rows, :] = ((on + bonus) * g).astype(BF16)
    for hd in range(H_A):
        s_ref[hd] = state[hd]


def _wkv_call(l, rw, rows, tt):
    bsz, seq, _ = rw.shape
    return pl.pallas_call(
        functools.partial(_wkv_kernel, tt),
        grid=(bsz, seq // tt),
        in_specs=[_tok_spec(tt, RW_COLS), _layer_spec(rows, l)],
        out_specs=_tok_spec(tt, D_MIX),
        out_shape=jax.ShapeDtypeStruct((bsz, seq, D_MIX), BF16),
        scratch_shapes=[pltpu.VMEM((H_A, HEAD_A, HEAD_A), F32)],
        compiler_params=_cparams(("parallel", "arbitrary")),
        name="wkv7",
    )(rw, rows)


def _s5_param_kernel(are_ref, aim_ref, ldt_ref, bre_ref, bim_ref, abre_ref, abim_ref, bbre_ref, bbim_ref):
    lam_re = jnp.minimum(are_ref[...], -1e-4)
    lam_im = aim_ref[...]
    dt = jnp.exp(ldt_ref[...])
    mag = jnp.exp(lam_re * dt)
    ab_re = mag * jnp.cos(lam_im * dt)
    ab_im = mag * jnp.sin(lam_im * dt)
    den = lam_re * lam_re + lam_im * lam_im
    q_re = ((ab_re - 1.0) * lam_re + ab_im * lam_im) / den
    q_im = (ab_im * lam_re - (ab_re - 1.0) * lam_im) / den
    abre_ref[...] = ab_re
    abim_ref[...] = ab_im
    b_re = bre_ref[...]
    b_im = bim_ref[...]
    bbre_ref[...] = q_re * b_re - q_im * b_im
    bbim_ref[...] = q_re * b_im + q_im * b_re


def _s5_param_call(a_re, a_im, log_dt, b_re, b_im):
    rows = a_re.size
    col = lambda a: a.reshape(rows, 1).astype(F32)
    ldt = jnp.repeat(log_dt.astype(F32), N_STATE, axis=-1).reshape(rows, 1)
    return pl.pallas_call(
        _s5_param_kernel,
        out_shape=(jax.ShapeDtypeStruct((rows, 1), F32), jax.ShapeDtypeStruct((rows, 1), F32),
                   jax.ShapeDtypeStruct((rows, GROUP_D), F32), jax.ShapeDtypeStruct((rows, GROUP_D), F32)),
        name="s5_params",
    )(col(a_re), col(a_im), ldt, b_re.reshape(rows, GROUP_D).astype(F32),
      b_im.reshape(rows, GROUP_D).astype(F32))


def _s5_kernel(tt, u_ref, perm_ref, bblk_ref, cre_ref, cim_ref, ab_ref, rows_ref, o_ref,
               bu_s, x_s, carry_s):
    half = S5_STATES
    steps = tt // S5_SUBSEQ

    @pl.when(pl.program_id(1) == 0)
    def _():
        carry_s[...] = jnp.zeros(carry_s.shape, F32)

    perm = perm_ref[...]
    u_p = _dot_sel(perm, u_ref[0])
    bu_s[...] = jnp.dot(u_p.astype(BF16), bblk_ref[0], preferred_element_type=F32)

    ab_re = ab_ref[0, 0:1, :]
    ab_im = ab_ref[0, 1:2, :]
    a_re = jnp.broadcast_to(ab_re, (S5_SUBSEQ, half))
    a_im = jnp.broadcast_to(ab_im, (S5_SUBSEQ, half))

    def step(i, st):
        s_re, s_im = st
        rows = pl.ds(pl.multiple_of(i * S5_SUBSEQ, S5_SUBSEQ), S5_SUBSEQ)
        n_re = a_re * s_re - a_im * s_im + bu_s[rows, 0:half]
        n_im = a_re * s_im + a_im * s_re + bu_s[rows, half:]
        return n_re, n_im

    zero = jnp.zeros((S5_SUBSEQ, half), F32)
    e_re, e_im = lax.fori_loop(0, steps, step, (zero, zero))

    p_re, p_im = ab_re, ab_im
    for _ in range(int(math.log2(steps))):
        p_re, p_im = p_re * p_re - p_im * p_im, 2.0 * p_re * p_im

    c_re, c_im = carry_s[0:1, :], carry_s[1:2, :]
    in_re, in_im = [], []
    for j in range(S5_SUBSEQ):
        in_re.append(c_re)
        in_im.append(c_im)
        c_re, c_im = (p_re * c_re - p_im * c_im + e_re[j:j + 1, :],
                      p_re * c_im + p_im * c_re + e_im[j:j + 1, :])
    carry_s[0:1, :] = c_re
    carry_s[1:2, :] = c_im
    i_re = jnp.concatenate(in_re, axis=0)
    i_im = jnp.concatenate(in_im, axis=0)

    def step2(i, st):
        n_re, n_im = step(i, st)
        rows = pl.ds(pl.multiple_of(i * S5_SUBSEQ, S5_SUBSEQ), S5_SUBSEQ)
        x_s[rows, 0:half] = n_re
        x_s[rows, half:] = n_im
        return n_re, n_im

    lax.fori_loop(0, steps, step2, (i_re, i_im))

    y = (jnp.dot(x_s[:, 0:half].astype(BF16), cre_ref[0], preferred_element_type=F32)
         - jnp.dot(x_s[:, half:].astype(BF16), cim_ref[0], preferred_element_type=F32))
    f_p = _gelu_tanh(y + rows_ref[0, ROW_S5D:ROW_S5D + 1, :] * u_p).astype(BF16)
    o_ref[0] = lax.dot_general(perm, f_p, (((0,), (0,)), ((), ())),
                               preferred_element_type=F32).astype(BF16)


def _s5_call(l, u5, perm, bblk, cre, cim, ab, rows, tt):
    bsz, seq, _ = u5.shape
    layer = [bblk, cre, cim, ab, rows]
    return pl.pallas_call(
        functools.partial(_s5_kernel, tt),
        grid=(bsz, seq // tt),
        in_specs=[_tok_spec(tt, D_MIX), _const_spec(perm)] + [_layer_spec(a, l) for a in layer],
        out_specs=_tok_spec(tt, D_MIX),
        out_shape=jax.ShapeDtypeStruct((bsz, seq, D_MIX), BF16),
        scratch_shapes=[pltpu.VMEM((tt, S5_LANES), F32), pltpu.VMEM((tt, S5_LANES), F32),
                        pltpu.VMEM((8, S5_STATES), F32)],
        compiler_params=_cparams(("parallel", "arbitrary")),
        name="s5_scan",
    )(u5, perm, *layer)


def _merge_kernel(x_ref, mod_ref, ng_ref, fa_ref, fb_ref, fc_ref, fd_ref, wg_ref, wout_ref, glu_ref,
                  wo_ref, o_ref):
    d = D_MODEL
    mod = mod_ref[0, 0]
    x = x_ref[0]
    hb = _rms_mod(x, ng_ref[0], mod[1:2, :], mod[0:1, :]).astype(BF16)

    def gate(i):
        return _sigmoid(jnp.dot(hb, wg_ref[0, :, i * d:(i + 1) * d], preferred_element_type=F32))

    merged = gate(0) * jnp.dot(fa_ref[0], wout_ref[0, 0], preferred_element_type=F32)
    merged += gate(1) * jnp.dot(fb_ref[0], wout_ref[0, 1], preferred_element_type=F32)
    merged += gate(2) * jnp.dot(fc_ref[0], wout_ref[0, 2], preferred_element_type=F32)
    hd = jnp.dot(fd_ref[0], glu_ref[0], preferred_element_type=F32)
    merged += gate(3) * (hd[:, :d] * _sigmoid(hd[:, d:]))
    o_ref[0] = x + mod[2:3, :] * jnp.dot(merged.astype(BF16), wo_ref[0], preferred_element_type=F32)


def _merge_call(l, x, mod4, ng, fa, fb, fc, fd, wg, wout, glu, wo, tm):
    bsz, seq, d = x.shape
    feat = _tok_spec(tm, D_MIX)
    return pl.pallas_call(
        _merge_kernel,
        grid=(bsz, seq // tm),
        in_specs=[_tok_spec(tm, d), _mod_spec(mod4, l), _layer_spec(ng, l), feat, feat, feat, feat,
                  _layer_spec(wg, l), _layer_spec(wout, l), _layer_spec(glu, l), _layer_spec(wo, l)],
        out_specs=_tok_spec(tm, d),
        out_shape=jax.ShapeDtypeStruct((bsz, seq, d), F32),
        compiler_params=_cparams(("parallel", "parallel")),
        name="merge",
    )(x, mod4, ng, fa, fb, fc, fd, wg, wout, glu, wo)


def _ffn_kernel(final, x_ref, mod_ref, ng_ref, w1_ref, w2_ref, fg_ref, o_ref):
    mod = mod_ref[0, 0]
    x = x_ref[0]
    hb = _rms_mod(x, ng_ref[0], mod[4:5, :], mod[3:4, :]).astype(BF16)
    acc = jnp.zeros(x.shape, F32)
    step = D_MODEL
    for j in range(D_FF // step):
        a = jnp.dot(hb, w1_ref[0, :, j * step:(j + 1) * step], preferred_element_type=F32)
        a = jnp.square(jnp.maximum(a, 0.0))
        acc += jnp.dot(a.astype(BF16), w2_ref[0, j * step:(j + 1) * step, :], preferred_element_type=F32)
    y = x + mod[5:6, :] * acc
    if final:
        ms = jnp.mean(y * y, axis=-1, keepdims=True)
        y = y * lax.rsqrt(ms + EPS) * fg_ref[...]
    o_ref[0] = y


def _ffn_call(l, x, mod4, ng, w1, w2, fg, final, tm):
    bsz, seq, d = x.shape
    return pl.pallas_call(
        functools.partial(_ffn_kernel, final),
        grid=(bsz, seq // tm),
        in_specs=[_tok_spec(tm, d), _mod_spec(mod4, l), _layer_spec(ng, l), _layer_spec(w1, l),
                  _layer_spec(w2, l), _const_spec(fg)],
        out_specs=_tok_spec(tm, d),
        out_shape=jax.ShapeDtypeStruct((bsz, seq, d), F32),
        compiler_params=_cparams(("parallel", "parallel")),
        name="ffn",
    )(x, mod4, ng, w1, w2, fg)


def _tile(seq, want):
    return want if seq % want == 0 else seq


def _rows3(a):
    return a.reshape(a.shape[0], 1, -1).astype(F32)


def _pad_lora(w, start):
    return jnp.pad(w, ((0, 0), (start, LORA_COLS - start - w.shape[1]), (0, 0)))


def kernel(x, c, ada_w, ada_b, norm_mix_g, w_in, rwkv_mu, rwkv_w0, rwkv_w2, rwkv_a0, rwkv_a2, rwkv_g2,
           rwkv_v0, rwkv_v1, rwkv_v2, rwkv_kk, rwkv_ka, rwkv_rk, rwkv_lnx_w, rwkv_lnx_b, rwkv_out,
           sg_ln_w, sg_ln_b, sg_ws, sg_bs, sg_out, conv_w, conv_out, s5_a_re, s5_a_im, s5_b_re, s5_b_im,
           s5_c_re, s5_c_im, s5_d, s5_log_dt, s5_glu_w, w_o, norm_ffn_g, ffn_w1, ffn_w2, final_g):
    in_dtype = x.dtype
    bsz, seq, d = x.shape
    depth = ada_w.shape[0]
    x = x.astype(F32)

    tt_front = _tile(seq, 512)
    tt_wkv = _tile(seq, 512)
    tt_s5 = _tile(seq, 256)
    tm = _tile(seq, 512)

    c_rows = 16
    c_pad = jnp.pad(c.astype(F32), ((0, c_rows - bsz), (0, 0)))
    mod4 = _ada_call(c_pad, ada_w.astype(F32), ada_b.astype(F32)).reshape(depth, c_rows, 6, d)

    zero_row = jnp.zeros((1, 1, D_MIX), F32)
    rows = jnp.concatenate(
        [_rows3(rwkv_w0), _rows3(rwkv_a0), _rows3(rwkv_kk), _rows3(rwkv_ka), _rows3(rwkv_rk),
         jnp.concatenate([zero_row, _rows3(rwkv_v0)], axis=0),
         _rows3(rwkv_lnx_w), _rows3(rwkv_lnx_b), _rows3(sg_ln_w), _rows3(sg_ln_b),
         conv_w.astype(F32), _rows3(s5_d),
         jnp.zeros((depth, N_ROWS - ROW_S5D - 1, D_MIX), F32)], axis=1)
    mu = _rows3(rwkv_mu)
    lora_w = jnp.concatenate([_pad_lora(rwkv_w2, 0), _pad_lora(rwkv_a2, LORA_W),
                              _pad_lora(rwkv_g2, LORA_W + LORA_A)], axis=2).astype(BF16)
    v1 = rwkv_v1.astype(BF16)
    v2 = rwkv_v2.astype(BF16)
    sgb = jnp.repeat(jnp.swapaxes(sg_bs, 1, 2), GROUP_B, axis=2).astype(F32)
    ws = sg_ws.astype(F32)
    win_s = w_in[:, :, :OFF_G].astype(BF16)
    wg = w_in[:, :, OFF_G:].astype(BF16)
    wout = jnp.stack([rwkv_out, sg_out, conv_out], axis=1).astype(BF16)
    glu = s5_glu_w.astype(BF16)
    wo = w_o.astype(BF16)
    w1 = ffn_w1.astype(BF16)
    w2 = ffn_w2.astype(BF16)
    ng_mix = _rows3(norm_mix_g)
    ng_ffn = _rows3(norm_ffn_g)
    fg = final_g.reshape(1, d).astype(F32)

    head_id = np.arange(D_MIX) // HEAD_A
    seg = jnp.asarray(head_id[:, None] == head_id[None, :], BF16)
    steps = tt_s5 // S5_SUBSEQ
    dst = np.arange(tt_s5)
    src = (dst % S5_SUBSEQ) * steps + dst // S5_SUBSEQ
    perm = jnp.asarray(src[:, None] == np.arange(tt_s5)[None, :], BF16)

    abre, abim, bbre, bbim = _s5_param_call(s5_a_re, s5_a_im, s5_log_dt, s5_b_re, s5_b_im)
    ab = jnp.concatenate([abre.reshape(depth, 1, S5_STATES), abim.reshape(depth, 1, S5_STATES)], axis=1)
    eye_g = jnp.asarray(np.eye(G_D), F32)

    def in_blk(bb):
        t = bb.reshape(depth, G_D, N_STATE, GROUP_D)
        return jnp.einsum('lgnc,gh->lgchn', t, eye_g).reshape(depth, D_MIX, S5_STATES)

    def out_blk(cc):
        return jnp.einsum('lgcn,gh->lgnhc', cc.astype(F32), eye_g).reshape(depth, S5_STATES, D_MIX).astype(BF16)

    bblk = jnp.concatenate([in_blk(bbre), in_blk(bbim)], axis=2).astype(BF16)
    cre = out_blk(s5_c_re)
    cim = out_blk(s5_c_im)

    rw0 = None
    for l in range(depth):
        rw, fb, fc, u5 = _front_call(l, x, mod4, ng_mix, win_s, mu, rows, lora_w, seg, ws, sgb, v1, v2,
                                     rw0, tt_front)
        if l == 0:
            rw0 = rw
        fa = _wkv_call(l, rw, rows, tt_wkv)
        fd = _s5_call(l, u5, perm, bblk, cre, cim, ab, rows, tt_s5)
        x = _merge_call(l, x, mod4, ng_mix, fa, fb, fc, fd, wg, wout, glu, wo, tm)
        x = _ffn_call(l, x, mod4, ng_ffn, w1, w2, fg, l == depth - 1, tm)
    return x.astype(in_dtype)
```

```python
import functools
import math

import numpy as np
import jax
import jax.numpy as jnp
from jax import lax
from jax.experimental import pallas as pl
from jax.experimental.pallas import tpu as pltpu

F32 = jnp.float32
BF16 = jnp.bfloat16

D_MODEL = 1024
N_BRANCH = 4
D_MIX = D_MODEL // N_BRANCH
HEAD_A = 64
H_A = D_MIX // HEAD_A
LORA_W = 32
LORA_A = 32
LORA_G = 64
LNX_EPS = 64e-5
CHUNK = 128
GROUP_B = 64
G_B = D_MIX // GROUP_B
CONV_K = 3
GROUP_D = 16
G_D = D_MIX // GROUP_D
N_STATE = 64
D_FF = 4 * D_MODEL
EPS = 1e-6
LN_EPS = 1e-5

A_COLS = 3 * D_MIX + LORA_W + LORA_A + LORA_G
B_COLS = 2 * D_MIX
C_COLS = 3 * D_MIX
D_COLS = D_MIX
OFF_B = A_COLS
OFF_C = OFF_B + B_COLS
OFF_D = OFF_C + C_COLS
OFF_G = OFF_D + D_COLS
LORA_COLS = LORA_W + LORA_A + LORA_G

WKV_CHUNK = 64
S5_STATES = G_D * N_STATE
S5_LANES = 2 * S5_STATES
S5_SUBSEQ = 8
S5_TILE = 256
RW_COLS = 8 * D_MIX

(ROW_W0, ROW_A0, ROW_KK, ROW_KA, ROW_RK, ROW_LNXW, ROW_LNXB, ROW_SGW, ROW_SGB, ROW_S5D) = range(10)

VMEM_LIMIT = 56 * 1024 * 1024


def _cparams(sem):
    return pltpu.CompilerParams(dimension_semantics=sem, vmem_limit_bytes=VMEM_LIMIT)


def _dot(a, b):
    return jnp.dot(a.astype(BF16), b.astype(BF16), preferred_element_type=F32)


def _dot_nt(a, b):
    return lax.dot_general(a.astype(BF16), b.astype(BF16), (((1,), (1,)), ((), ())),
                           preferred_element_type=F32)


def _dot_tn(a, b):
    return lax.dot_general(a.astype(BF16), b.astype(BF16), (((0,), (0,)), ((), ())),
                           preferred_element_type=F32)


def _split3(x):
    hi = x.astype(BF16)
    r1 = x - hi.astype(F32)
    mid = r1.astype(BF16)
    lo = (r1 - mid.astype(F32)).astype(BF16)
    return hi, mid, lo


def _dot_sel(sel, x):
    hi, mid, lo = _split3(x)
    return (jnp.dot(sel, hi, preferred_element_type=F32) + jnp.dot(sel, mid, preferred_element_type=F32)
            + jnp.dot(sel, lo, preferred_element_type=F32))


def _dot_x_sel(x, sel):
    hi, mid, lo = _split3(x)
    return (jnp.dot(hi, sel, preferred_element_type=F32) + jnp.dot(mid, sel, preferred_element_type=F32)
            + jnp.dot(lo, sel, preferred_element_type=F32))


def _dot3(a, b):
    ah, am, _ = _split3(a)
    bh, bm, _ = _split3(b)
    return (jnp.dot(ah, bh, preferred_element_type=F32) + jnp.dot(ah, bm, preferred_element_type=F32)
            + jnp.dot(am, bh, preferred_element_type=F32))


def _sigmoid(x):
    return 0.5 * jnp.tanh(0.5 * x) + 0.5


def _gelu_tanh(x):
    return 0.5 * x * (1.0 + jnp.tanh(math.sqrt(2.0 / math.pi) * (x + 0.044715 * (x * x * x))))


def _rms_mod(x, g, sc, sh):
    ms = jnp.mean(x * x, axis=-1, keepdims=True)
    return x * lax.rsqrt(ms + EPS) * g * (1.0 + sc) + sh


def _layer_spec(a, l):
    return pl.BlockSpec((1,) + a.shape[1:], lambda b, t: (l,) + (0,) * (a.ndim - 1))


def _mod_spec(mod4, l):
    return pl.BlockSpec((1, 1) + mod4.shape[2:], lambda b, t: (l, b, 0, 0))


def _const_spec(a):
    return pl.BlockSpec(a.shape, lambda b, t: (0,) * a.ndim)


def _tok_spec(tt, cols):
    return pl.BlockSpec((1, tt, cols), lambda b, t: (b, t, 0))


def _ada_kernel(c_ref, w_ref, b_ref, o_ref):
    c = c_ref[...]
    ca = c * _sigmoid(c)
    o_ref[0] = _dot3(ca, w_ref[0]) + b_ref[0]


def _ada_call(c_pad, ada_w, ada_b):
    depth, d, n = ada_w.shape
    rows = c_pad.shape[0]
    bn = 1536
    return pl.pallas_call(
        _ada_kernel,
        grid=(depth, n // bn),
        in_specs=[
            pl.BlockSpec((rows, d), lambda l, j: (0, 0)),
            pl.BlockSpec((1, d, bn), lambda l, j: (l, 0, j)),
            pl.BlockSpec((1, 1, bn), lambda l, j: (l, 0, j)),
        ],
        out_specs=pl.BlockSpec((1, rows, bn), lambda l, j: (l, 0, j)),
        out_shape=jax.ShapeDtypeStruct((depth, rows, n), F32),
        compiler_params=_cparams(("parallel", "parallel")),
        name="ada_mod",
    )(c_pad, ada_w, ada_b.reshape(depth, 1, n))


def _wkv_part(tt, rw_s, lnx_w, lnx_b, o_ref, s_ref):
    n = WKV_CHUNK
    nc = tt // n

    row = lax.broadcasted_iota(jnp.int32, (n, n), 0)
    col = lax.broadcasted_iota(jnp.int32, (n, n), 1)
    ltri = jnp.where(row >= col, 1.0, 0.0).astype(BF16)
    strict = row > col
    incl = row >= col
    eye = jnp.where(row == col, 1.0, 0.0)
    same = lambda s: jnp.right_shift(row, s) == jnp.right_shift(col, s)

    items = [(ci, hd) for ci in range(nc) for hd in range(H_A)]
    head = lambda x, hd: x[:, hd * HEAD_A:(hd + 1) * HEAD_A]

    a_t, r_t, b_t, k_t, b_h, k_h, vv, g_end = [], [], [], [], [], [], [], []
    for ci in range(nc):
        rows = slice(ci * n, (ci + 1) * n)
        r = rw_s[rows, 0 * D_MIX:1 * D_MIX]
        lw = rw_s[rows, 1 * D_MIX:2 * D_MIX]
        k = rw_s[rows, 2 * D_MIX:3 * D_MIX]
        kk = rw_s[rows, 4 * D_MIX:5 * D_MIX]
        b = rw_s[rows, 5 * D_MIX:6 * D_MIX]
        e = _dot_sel(ltri, lw)
        eg = jnp.exp(e)
        ig = jnp.exp(-e)
        ge = eg[n - 1:n, :]
        a_t.append((-kk * jnp.exp(e - lw)).astype(BF16))
        r_t.append(r * eg)
        b_t.append((b * ig).astype(BF16))
        k_t.append((k * ig).astype(BF16))
        b_h.append((b * ig * ge).astype(BF16))
        k_h.append((k * ig * ge).astype(BF16))
        vv.append(rw_s[rows, 3 * D_MIX:4 * D_MIX].astype(BF16))
        g_end.append(ge)

    gm = [_dot_nt(jnp.concatenate([head(a_t[ci], hd), head(r_t[ci], hd).astype(BF16)], axis=0),
                  jnp.concatenate([head(b_t[ci], hd), head(k_t[ci], hd)], axis=0)) for ci, hd in items]
    a_ab = [jnp.where(strict, m[:n, :n], 0.0) for m in gm]
    a_ak = [jnp.where(strict, m[:n, n:], 0.0) for m in gm]
    a_rb = [jnp.where(incl, m[n:, :n], 0.0) for m in gm]
    a_rk = [jnp.where(incl, m[n:, n:], 0.0) for m in gm]
    inv = [eye + jnp.where(same(1), a, 0.0) for a in a_ab]
    for s in range(2, int(math.log2(n)) + 1):
        off_mask = same(s) & jnp.logical_not(same(s - 1))
        tmp = [_dot(t, jnp.where(off_mask, a, 0.0)) for t, a in zip(inv, a_ab)]
        inv = [t + _dot(x, t) for t, x in zip(inv, tmp)]
    akv = [_dot(a_ak[i], head(vv[ci], hd)) for i, (ci, hd) in enumerate(items)]
    pq = [_dot(inv[i], jnp.concatenate([head(a_t[ci], hd), akv[i].astype(BF16)], axis=1))
          for i, (ci, hd) in enumerate(items)]
    mn = [_dot_tn(pq[i], head(b_h[ci], hd)) for i, (ci, hd) in enumerate(items)]
    kv = [_dot_tn(head(vv[ci], hd), head(k_h[ci], hd)) for ci, hd in items]
    rq = [_dot(a_rb[i], pq[i]) for i in range(len(items))]
    ark = [_dot(a_rk[i], head(vv[ci], hd)) for i, (ci, hd) in enumerate(items)]

    state = [s_ref[hd] for hd in range(H_A)]
    for ci in range(nc):
        outs = []
        for hd in range(H_A):
            i = ci * H_A + hd
            s0 = state[hd]
            ro = head(r_t[ci], hd) + rq[i][:, :n]
            o = _dot_nt(ro, s0) + rq[i][:, n:] + ark[i]
            state[hd] = s0 * head(g_end[ci], hd) + _dot(s0, mn[i][:n, :]) + (mn[i][n:, :] + kv[i])
            mu = jnp.mean(o, axis=-1, keepdims=True)
            var = jnp.mean(jnp.square(o - mu), axis=-1, keepdims=True)
            outs.append((o - mu) * lax.rsqrt(var + LNX_EPS))
        rows = slice(ci * n, (ci + 1) * n)
        on = jnp.concatenate(outs, axis=1) * lnx_w + lnx_b
        g = rw_s[rows, 6 * D_MIX:7 * D_MIX]
        bonus = rw_s[rows, 7 * D_MIX:8 * D_MIX]
        o_ref[0, rows, :] = ((on + bonus) * g).astype(BF16)
    for hd in range(H_A):
        s_ref[hd] = state[hd]


def _s5_part(u, perm, bblk, cre_ref, cim_ref, ab_re, ab_im, d_row, bu_s, x_s, carry_s):
    half = S5_STATES
    steps = S5_TILE // S5_SUBSEQ

    u_p = _dot_sel(perm, u)
    bu_s[...] = jnp.dot(u_p.astype(BF16), bblk, preferred_element_type=F32)

    a_re = jnp.broadcast_to(ab_re, (S5_SUBSEQ, half))
    a_im = jnp.broadcast_to(ab_im, (S5_SUBSEQ, half))

    def step(i, st):
        s_re, s_im = st
        rows = pl.ds(pl.multiple_of(i * S5_SUBSEQ, S5_SUBSEQ), S5_SUBSEQ)
        n_re = a_re * s_re - a_im * s_im + bu_s[rows, 0:half]
        n_im = a_re * s_im + a_im * s_re + bu_s[rows, half:]
        return n_re, n_im

    zero = jnp.zeros((S5_SUBSEQ, half), F32)
    e_re, e_im = lax.fori_loop(0, steps, step, (zero, zero))

    p_re, p_im = ab_re, ab_im
    for _ in range(int(math.log2(steps))):
        p_re, p_im = p_re * p_re - p_im * p_im, 2.0 * p_re * p_im

    c_re, c_im = carry_s[0:1, :], carry_s[1:2, :]
    in_re, in_im = [], []
    for j in range(S5_SUBSEQ):
        in_re.append(c_re)
        in_im.append(c_im)
        c_re, c_im = (p_re * c_re - p_im * c_im + e_re[j:j + 1, :],
                      p_re * c_im + p_im * c_re + e_im[j:j + 1, :])
    carry_s[0:1, :] = c_re
    carry_s[1:2, :] = c_im

    def step2(i, st):
        n_re, n_im = step(i, st)
        rows = pl.ds(pl.multiple_of(i * S5_SUBSEQ, S5_SUBSEQ), S5_SUBSEQ)
        x_s[rows, 0:half] = n_re
        x_s[rows, half:] = n_im
        return n_re, n_im

    lax.fori_loop(0, steps, step2, (jnp.concatenate(in_re, axis=0), jnp.concatenate(in_im, axis=0)))

    y = (jnp.dot(x_s[:, 0:half].astype(BF16), cre_ref[0], preferred_element_type=F32)
         - jnp.dot(x_s[:, half:].astype(BF16), cim_ref[0], preferred_element_type=F32))
    f_p = _gelu_tanh(y + d_row * u_p).astype(BF16)
    return lax.dot_general(perm, f_p, (((0,), (0,)), ((), ())), preferred_element_type=F32).astype(BF16)


def _mix_kernel(l, tt, *refs):
    has_vmix = l > 0
    refs = list(refs)
    (x_ref, mod_ref, ng_ref, win_ref, mu_ref, rows_ref, w2_ref, a2_ref, g2_ref, seg_ref, ws_ref, sgb_ref,
     cw_ref, perm_ref, bblk_ref, cre_ref, cim_ref, ab_ref) = refs[:18]
    refs = refs[18:]
    if has_vmix:
        vf_ref, v0_ref, v1_ref, v2_ref = refs[:4]
        refs = refs[4:]
        fa_ref, fb_ref, fc_ref, fd_ref = refs[:4]
        refs = refs[4:]
    else:
        fa_ref, fb_ref, fc_ref, fd_ref, vout_ref = refs[:5]
        refs = refs[5:]
    pa_s, z_s, rw_s, s_ref, bu_s, x_s, carry_s = refs
    prow = lambda i: rows_ref[i, l:l + 1, :]

    @pl.when(pl.program_id(1) == 0)
    def _():
        pa_s[0:8, :] = jnp.zeros((8, A_COLS), F32)
        z_s[0:8, :] = jnp.zeros((8, D_MIX), F32)
        s_ref[...] = jnp.zeros(s_ref.shape, F32)
        carry_s[...] = jnp.zeros(carry_s.shape, F32)

    mod = mod_ref[0, 0]
    h = _rms_mod(x_ref[0], ng_ref[l:l + 1, :], mod[1:2, :], mod[0:1, :])
    p = jnp.dot(h.astype(BF16), win_ref[0], preferred_element_type=F32)

    pa_s[8:8 + tt, :] = p[:, :A_COLS]
    pa = p[:, :A_COLS]
    prev = pa_s[7:7 + tt, :]
    pa = pa + (prev - pa) * mu_ref[l:l + 1, :]
    pa_s[0:8, :] = pa_s[tt:tt + 8, :]
    r = pa[:, 0:D_MIX]
    k = pa[:, D_MIX:2 * D_MIX]
    v = pa[:, 2 * D_MIX:3 * D_MIX]
    lora = pa[:, 3 * D_MIX:A_COLS]

    def lora_w(w_ref, start):
        w = w_ref[0]
        parts = []
        if start:
            parts.append(jnp.zeros((start, D_MIX), F32))
        parts.append(w)
        if LORA_COLS - start - w.shape[0]:
            parts.append(jnp.zeros((LORA_COLS - start - w.shape[0], D_MIX), F32))
        return jnp.concatenate(parts, axis=0).astype(BF16)

    lw = -math.exp(-0.5) * _sigmoid(prow(ROW_W0) + _dot(jnp.tanh(lora), lora_w(w2_ref, 0)))
    if has_vmix:
        vgate = _sigmoid(v0_ref[l - 1:l, :] + _dot(_dot(v, v1_ref[0]), v2_ref[0]))
        v = v + (vf_ref[0] - v) * vgate
    else:
        vout_ref[0] = v
    a = _sigmoid(prow(ROW_A0) + _dot(lora, lora_w(a2_ref, LORA_W)))
    g = _dot(_sigmoid(lora), lora_w(g2_ref, LORA_W + LORA_A))
    kk = k * prow(ROW_KK)
    seg = seg_ref[...]
    kk_norm = jnp.sqrt(_dot_x_sel(kk * kk, seg))
    kk = kk / jnp.maximum(kk_norm, 1e-12)
    k = k * (1.0 + (a - 1.0) * prow(ROW_KA))
    bonus = _dot_x_sel(r * k * prow(ROW_RK), seg) * v
    rw_s[:, 0 * D_MIX:1 * D_MIX] = r
    rw_s[:, 1 * D_MIX:2 * D_MIX] = lw
    rw_s[:, 2 * D_MIX:3 * D_MIX] = k
    rw_s[:, 3 * D_MIX:4 * D_MIX] = v
    rw_s[:, 4 * D_MIX:5 * D_MIX] = kk
    rw_s[:, 5 * D_MIX:6 * D_MIX] = kk * a
    rw_s[:, 6 * D_MIX:7 * D_MIX] = g
    rw_s[:, 7 * D_MIX:8 * D_MIX] = bonus

    z = _gelu_tanh(p[:, OFF_B:OFF_C])
    su = z[:, :D_MIX]
    sv = z[:, D_MIX:]
    mu_v = jnp.mean(sv, axis=-1, keepdims=True)
    var_v = jnp.mean(jnp.square(sv - mu_v), axis=-1, keepdims=True)
    sv = (sv - mu_v) * lax.rsqrt(var_v + LN_EPS) * prow(ROW_SGW) + prow(ROW_SGB)
    row = lax.broadcasted_iota(jnp.int32, (CHUNK, CHUNK), 0)
    col = lax.broadcasted_iota(jnp.int32, (CHUNK, CHUNK), 1)
    causal = row >= col
    wsm = [jnp.where(causal, ws_ref[0, gi], 0.0).astype(BF16) for gi in range(G_B)]
    sv_b = sv.astype(BF16)
    for n in range(tt // CHUNK):
        rows = slice(n * CHUNK, (n + 1) * CHUNK)
        mixed = jnp.concatenate(
            [jnp.dot(wsm[gi], sv_b[rows, gi * GROUP_B:(gi + 1) * GROUP_B], preferred_element_type=F32)
             for gi in range(G_B)], axis=1) + sgb_ref[0]
        fb_ref[0, rows, :] = (su[rows, :] * mixed).astype(BF16)

    pc = p[:, OFF_C:OFF_D]
    bg = pc[:, :D_MIX]
    zc = pc[:, D_MIX:2 * D_MIX] * pc[:, 2 * D_MIX:]
    z_s[8:8 + tt, :] = zc
    y = (cw_ref[l, 0:1, :] * z_s[6:6 + tt, :] + cw_ref[l, 1:2, :] * z_s[7:7 + tt, :]
         + cw_ref[l, 2:3, :] * zc)
    z_s[0:8, :] = z_s[tt:tt + 8, :]
    fc_ref[0] = (bg * y).astype(BF16)

    u5 = p[:, OFF_D:OFF_G]
    for n in range(tt // S5_TILE):
        rows = slice(n * S5_TILE, (n + 1) * S5_TILE)
        fd_ref[0, rows, :] = _s5_part(u5[rows, :], perm_ref[...], bblk_ref[0], cre_ref, cim_ref,
                                      ab_ref[0, 0:1, :], ab_ref[0, 1:2, :], prow(ROW_S5D),
                                      bu_s, x_s, carry_s)

    _wkv_part(tt, rw_s, prow(ROW_LNXW), prow(ROW_LNXB), fa_ref, s_ref)


def _mix_call(l, x, mod4, ng, win, mu, rows, w2, a2, g2, seg, ws, sgb, cw, perm, bblk, cre, cim, ab,
              v0, v1, v2, v_first, tt):
    bsz, seq, d = x.shape
    has_vmix = l > 0
    ins = [x, mod4, ng, win, mu, rows, w2, a2, g2, seg, ws, sgb, cw, perm, bblk, cre, cim, ab]
    in_specs = [
        _tok_spec(tt, d), _mod_spec(mod4, l), _const_spec(ng),
        pl.BlockSpec((1, d, OFF_G), lambda b, t: (l, 0, 0)),
        _const_spec(mu), _const_spec(rows), _layer_spec(w2, l), _layer_spec(a2, l), _layer_spec(g2, l),
        _const_spec(seg), _layer_spec(ws, l), _layer_spec(sgb, l), _const_spec(cw), _const_spec(perm),
        _layer_spec(bblk, l), _layer_spec(cre, l), _layer_spec(cim, l), _layer_spec(ab, l),
    ]
    feat = jax.ShapeDtypeStruct((bsz, seq, D_MIX), BF16)
    out_shape = [feat, feat, feat, feat]
    out_specs = [_tok_spec(tt, D_MIX)] * 4
    if has_vmix:
        ins += [v_first, v0, v1, v2]
        in_specs += [_tok_spec(tt, D_MIX), _const_spec(v0), _layer_spec(v1, l - 1), _layer_spec(v2, l - 1)]
    else:
        out_shape.append(jax.ShapeDtypeStruct((bsz, seq, D_MIX), F32))
        out_specs.append(_tok_spec(tt, D_MIX))
    return pl.pallas_call(
        functools.partial(_mix_kernel, l, tt),
        grid=(bsz, seq // tt),
        in_specs=in_specs,
        out_specs=out_specs,
        out_shape=out_shape,
        scratch_shapes=[pltpu.VMEM((tt + 8, A_COLS), F32), pltpu.VMEM((tt + 8, D_MIX), F32),
                        pltpu.VMEM((tt, RW_COLS), F32), pltpu.VMEM((H_A, HEAD_A, HEAD_A), F32),
                        pltpu.VMEM((S5_TILE, S5_LANES), F32), pltpu.VMEM((S5_TILE, S5_LANES), F32),
                        pltpu.VMEM((8, S5_STATES), F32)],
        compiler_params=_cparams(("parallel", "arbitrary")),
        name="mix",
    )(*ins)


def _s5_param_kernel(are_ref, aim_ref, ldt_ref, bre_ref, bim_ref, abre_ref, abim_ref, bbre_ref, bbim_ref):
    lam_re = jnp.minimum(are_ref[...], -1e-4)
    lam_im = aim_ref[...]
    dt = jnp.exp(ldt_ref[...])
    mag = jnp.exp(lam_re * dt)
    ab_re = mag * jnp.cos(lam_im * dt)
    ab_im = mag * jnp.sin(lam_im * dt)
    den = lam_re * lam_re + lam_im * lam_im
    q_re = ((ab_re - 1.0) * lam_re + ab_im * lam_im) / den
    q_im = (ab_im * lam_re - (ab_re - 1.0) * lam_im) / den
    abre_ref[...] = ab_re
    abim_ref[...] = ab_im
    b_re = bre_ref[...]
    b_im = bim_ref[...]
    bbre_ref[...] = q_re * b_re - q_im * b_im
    bbim_ref[...] = q_re * b_im + q_im * b_re


def _s5_param_call(a_re, a_im, log_dt, b_re, b_im):
    rows = a_re.size
    col = lambda a: a.reshape(rows, 1).astype(F32)
    ldt = jnp.repeat(log_dt.astype(F32), N_STATE, axis=-1).reshape(rows, 1)
    return pl.pallas_call(
        _s5_param_kernel,
        out_shape=(jax.ShapeDtypeStruct((rows, 1), F32), jax.ShapeDtypeStruct((rows, 1), F32),
                   jax.ShapeDtypeStruct((rows, GROUP_D), F32), jax.ShapeDtypeStruct((rows, GROUP_D), F32)),
        name="s5_params",
    )(col(a_re), col(a_im), ldt, b_re.reshape(rows, GROUP_D).astype(F32),
      b_im.reshape(rows, GROUP_D).astype(F32))


def _merge_kernel(l, x_ref, mod_ref, ng_ref, fa_ref, fb_ref, fc_ref, fd_ref, wg_ref, wa_ref, wb_ref, wc_ref,
                  glu_ref, wo_ref, o_ref):
    d = D_MODEL
    mod = mod_ref[0, 0]
    x = x_ref[0]
    hb = _rms_mod(x, ng_ref[l:l + 1, :], mod[1:2, :], mod[0:1, :]).astype(BF16)

    def gate(i):
        return _sigmoid(jnp.dot(hb, wg_ref[0, :, i * d:(i + 1) * d], preferred_element_type=F32))

    merged = gate(0) * jnp.dot(fa_ref[0], wa_ref[0], preferred_element_type=F32)
    merged += gate(1) * jnp.dot(fb_ref[0], wb_ref[0], preferred_element_type=F32)
    merged += gate(2) * jnp.dot(fc_ref[0], wc_ref[0], preferred_element_type=F32)
    hd = jnp.dot(fd_ref[0], glu_ref[0], preferred_element_type=F32)
    merged += gate(3) * (hd[:, :d] * _sigmoid(hd[:, d:]))
    o_ref[0] = x + mod[2:3, :] * jnp.dot(merged.astype(BF16), wo_ref[0], preferred_element_type=F32)


def _merge_call(l, x, mod4, ng, fa, fb, fc, fd, wg, wa, wb, wc, glu, wo, tm):
    bsz, seq, d = x.shape
    feat = _tok_spec(tm, D_MIX)
    return pl.pallas_call(
        functools.partial(_merge_kernel, l),
        grid=(bsz, seq // tm),
        in_specs=[_tok_spec(tm, d), _mod_spec(mod4, l), _const_spec(ng), feat, feat, feat, feat,
                  _layer_spec(wg, l), _layer_spec(wa, l), _layer_spec(wb, l), _layer_spec(wc, l),
                  _layer_spec(glu, l), _layer_spec(wo, l)],
        out_specs=_tok_spec(tm, d),
        out_shape=jax.ShapeDtypeStruct((bsz, seq, d), F32),
        compiler_params=_cparams(("parallel", "parallel")),
        name="merge",
    )(x, mod4, ng, fa, fb, fc, fd, wg, wa, wb, wc, glu, wo)


def _ffn_kernel(l, final, x_ref, mod_ref, ng_ref, w1_ref, w2_ref, fg_ref, o_ref):
    mod = mod_ref[0, 0]
    x = x_ref[0]
    hb = _rms_mod(x, ng_ref[l:l + 1, :], mod[4:5, :], mod[3:4, :]).astype(BF16)
    acc = jnp.zeros(x.shape, F32)
    step = D_MODEL
    for j in range(D_FF // step):
        a = jnp.dot(hb, w1_ref[0, :, j * step:(j + 1) * step], preferred_element_type=F32)
        a = jnp.square(jnp.maximum(a, 0.0))
        acc += jnp.dot(a.astype(BF16), w2_ref[0, j * step:(j + 1) * step, :], preferred_element_type=F32)
    y = x + mod[5:6, :] * acc
    if final:
        ms = jnp.mean(y * y, axis=-1, keepdims=True)
        y = y * lax.rsqrt(ms + EPS) * fg_ref[...]
    o_ref[0] = y


def _ffn_call(l, x, mod4, ng, w1, w2, fg, final, tm):
    bsz, seq, d = x.shape
    return pl.pallas_call(
        functools.partial(_ffn_kernel, l, final),
        grid=(bsz, seq // tm),
        in_specs=[_tok_spec(tm, d), _mod_spec(mod4, l), _const_spec(ng), _layer_spec(w1, l),
                  _layer_spec(w2, l), _const_spec(fg)],
        out_specs=_tok_spec(tm, d),
        out_shape=jax.ShapeDtypeStruct((bsz, seq, d), F32),
        compiler_params=_cparams(("parallel", "parallel")),
        name="ffn",
    )(x, mod4, ng, w1, w2, fg)


def _tile(seq, want):
    return want if seq % want == 0 else seq


def kernel(x, c, ada_w, ada_b, norm_mix_g, w_in, rwkv_mu, rwkv_w0, rwkv_w2, rwkv_a0, rwkv_a2, rwkv_g2,
           rwkv_v0, rwkv_v1, rwkv_v2, rwkv_kk, rwkv_ka, rwkv_rk, rwkv_lnx_w, rwkv_lnx_b, rwkv_out,
           sg_ln_w, sg_ln_b, sg_ws, sg_bs, sg_out, conv_w, conv_out, s5_a_re, s5_a_im, s5_b_re, s5_b_im,
           s5_c_re, s5_c_im, s5_d, s5_log_dt, s5_glu_w, w_o, norm_ffn_g, ffn_w1, ffn_w2, final_g):
    in_dtype = x.dtype
    bsz, seq, d = x.shape
    depth = ada_w.shape[0]
    x = x.astype(F32)

    tt = _tile(seq, 512)
    tm = _tile(seq, 512)

    c_rows = 16
    c_pad = jnp.pad(c.astype(F32), ((0, c_rows - bsz), (0, 0)))
    mod4 = _ada_call(c_pad, ada_w.astype(F32), ada_b.astype(F32)).reshape(depth, c_rows, 6, d)

    rows = jnp.stack([rwkv_w0, rwkv_a0, rwkv_kk, rwkv_ka, rwkv_rk.reshape(depth, D_MIX), rwkv_lnx_w,
                      rwkv_lnx_b, sg_ln_w, sg_ln_b, s5_d], axis=0).astype(F32)
    sgb = jnp.repeat(jnp.swapaxes(sg_bs, 1, 2), GROUP_B, axis=2).astype(F32)
    win = w_in.astype(BF16)
    wg = win[:, :, OFF_G:]
    wa, wb, wc = rwkv_out.astype(BF16), sg_out.astype(BF16), conv_out.astype(BF16)
    glu = s5_glu_w.astype(BF16)
    wo = w_o.astype(BF16)
    w1 = ffn_w1.astype(BF16)
    w2 = ffn_w2.astype(BF16)
    fg = final_g.reshape(1, d).astype(F32)

    head_id = np.arange(D_MIX) // HEAD_A
    seg = jnp.asarray(head_id[:, None] == head_id[None, :], BF16)
    steps = S5_TILE // S5_SUBSEQ
    dst = np.arange(S5_TILE)
    src = (dst % S5_SUBSEQ) * steps + dst // S5_SUBSEQ
    perm = jnp.asarray(src[:, None] == np.arange(S5_TILE)[None, :], BF16)

    abre, abim, bbre, bbim = _s5_param_call(s5_a_re, s5_a_im, s5_log_dt, s5_b_re, s5_b_im)
    ab = jnp.concatenate([abre.reshape(depth, 1, S5_STATES), abim.reshape(depth, 1, S5_STATES)], axis=1)
    eye_g = jnp.asarray(np.eye(G_D), F32)

    def in_blk(bb):
        t = bb.reshape(depth, G_D, N_STATE, GROUP_D)
        return jnp.einsum('lgnc,gh->lgchn', t, eye_g).reshape(depth, D_MIX, S5_STATES)

    def out_blk(cc):
        return jnp.einsum('lgcn,gh->lgnhc', cc.astype(F32), eye_g).reshape(depth, S5_STATES, D_MIX).astype(BF16)

    bblk = jnp.concatenate([in_blk(bbre), in_blk(bbim)], axis=2).astype(BF16)
    cre = out_blk(s5_c_re)
    cim = out_blk(s5_c_im)

    f32 = lambda a: a.astype(F32)
    v_first = None
    for l in range(depth):
        outs = _mix_call(l, x, mod4, f32(norm_mix_g), win, f32(rwkv_mu), rows, f32(rwkv_w2), f32(rwkv_a2),
                         f32(rwkv_g2), seg, f32(sg_ws), sgb, f32(conv_w), perm, bblk, cre, cim, ab,
                         f32(rwkv_v0), f32(rwkv_v1), f32(rwkv_v2), v_first, tt)
        fa, fb, fc, fd = outs[:4]
        if l == 0:
            v_first = outs[4]
        x = _merge_call(l, x, mod4, f32(norm_mix_g), fa, fb, fc, fd, wg, wa, wb, wc, glu, wo, tm)
        x = _ffn_call(l, x, mod4, f32(norm_ffn_g), w1, w2, fg, l == depth - 1, tm)
    return x.astype(in_dtype)
```

```python
import functools
import math

import numpy as np
import jax
import jax.numpy as jnp
from jax import lax
from jax.experimental import pallas as pl
from jax.experimental.pallas import tpu as pltpu

F32 = jnp.float32
BF16 = jnp.bfloat16

D_MODEL = 1024
N_BRANCH = 4
D_MIX = D_MODEL // N_BRANCH
HEAD_A = 64
H_A = D_MIX // HEAD_A
LORA_W = 32
LORA_A = 32
LORA_G = 64
LNX_EPS = 64e-5
CHUNK = 128
GROUP_B = 64
G_B = D_MIX // GROUP_B
CONV_K = 3
GROUP_D = 16
G_D = D_MIX // GROUP_D
N_STATE = 64
D_FF = 4 * D_MODEL
EPS = 1e-6
LN_EPS = 1e-5

A_COLS = 3 * D_MIX + LORA_W + LORA_A + LORA_G
B_COLS = 2 * D_MIX
C_COLS = 3 * D_MIX
D_COLS = D_MIX
OFF_B = A_COLS
OFF_C = OFF_B + B_COLS
OFF_D = OFF_C + C_COLS
OFF_G = OFF_D + D_COLS
LORA_COLS = LORA_W + LORA_A + LORA_G

WKV_CHUNK = 64
S5_STATES = G_D * N_STATE
S5_LANES = 2 * S5_STATES
S5_SUBSEQ = 8
S5_TILE = 256
RW_COLS = 8 * D_MIX

(ROW_W0, ROW_A0, ROW_KK, ROW_KA, ROW_RK, ROW_LNXW, ROW_LNXB, ROW_SGW, ROW_SGB, ROW_S5D) = range(10)

VMEM_LIMIT = 56 * 1024 * 1024


def _cparams(sem):
    return pltpu.CompilerParams(dimension_semantics=sem, vmem_limit_bytes=VMEM_LIMIT)


def _dot(a, b):
    return jnp.dot(a.astype(BF16), b.astype(BF16), preferred_element_type=F32)


def _dot_nt(a, b):
    return lax.dot_general(a.astype(BF16), b.astype(BF16), (((1,), (1,)), ((), ())),
                           preferred_element_type=F32)


def _dot_tn(a, b):
    return lax.dot_general(a.astype(BF16), b.astype(BF16), (((0,), (0,)), ((), ())),
                           preferred_element_type=F32)


def _split3(x):
    hi = x.astype(BF16)
    r1 = x - hi.astype(F32)
    mid = r1.astype(BF16)
    lo = (r1 - mid.astype(F32)).astype(BF16)
    return hi, mid, lo


def _dot_sel(sel, x):
    hi, mid, lo = _split3(x)
    return (jnp.dot(sel, hi, preferred_element_type=F32) + jnp.dot(sel, mid, preferred_element_type=F32)
            + jnp.dot(sel, lo, preferred_element_type=F32))


def _dot_x_sel(x, sel):
    hi, mid, lo = _split3(x)
    return (jnp.dot(hi, sel, preferred_element_type=F32) + jnp.dot(mid, sel, preferred_element_type=F32)
            + jnp.dot(lo, sel, preferred_element_type=F32))


def _dot3(a, b):
    ah, am, _ = _split3(a)
    bh, bm, _ = _split3(b)
    return (jnp.dot(ah, bh, preferred_element_type=F32) + jnp.dot(ah, bm, preferred_element_type=F32)
            + jnp.dot(am, bh, preferred_element_type=F32))


def _sigmoid(x):
    return 0.5 * jnp.tanh(0.5 * x) + 0.5


def _gelu_tanh(x):
    return 0.5 * x * (1.0 + jnp.tanh(math.sqrt(2.0 / math.pi) * (x + 0.044715 * (x * x * x))))


def _rms_mod(x, g, sc, sh):
    ms = jnp.mean(x * x, axis=-1, keepdims=True)
    return x * lax.rsqrt(ms + EPS) * g * (1.0 + sc) + sh


def _layer_spec(a, l):
    return pl.BlockSpec((1,) + a.shape[1:], lambda b, t: (l,) + (0,) * (a.ndim - 1))


def _mod_spec(mod4, l):
    return pl.BlockSpec((1, 1) + mod4.shape[2:], lambda b, t: (l, b, 0, 0))


def _const_spec(a):
    return pl.BlockSpec(a.shape, lambda b, t: (0,) * a.ndim)


def _tok_spec(tt, cols):
    return pl.BlockSpec((1, tt, cols), lambda b, t: (b, t, 0))


def _ada_kernel(c_ref, w_ref, b_ref, o_ref):
    c = c_ref[...]
    ca = c * _sigmoid(c)
    o_ref[0] = _dot3(ca, w_ref[0]) + b_ref[0]


def _ada_call(c_pad, ada_w, ada_b):
    depth, d, n = ada_w.shape
    rows = c_pad.shape[0]
    bn = 1536
    return pl.pallas_call(
        _ada_kernel,
        grid=(depth, n // bn),
        in_specs=[
            pl.BlockSpec((rows, d), lambda l, j: (0, 0)),
            pl.BlockSpec((1, d, bn), lambda l, j: (l, 0, j)),
            pl.BlockSpec((1, 1, bn), lambda l, j: (l, 0, j)),
        ],
        out_specs=pl.BlockSpec((1, rows, bn), lambda l, j: (l, 0, j)),
        out_shape=jax.ShapeDtypeStruct((depth, rows, n), F32),
        compiler_params=_cparams(("parallel", "parallel")),
        name="ada_mod",
    )(c_pad, ada_w, ada_b.reshape(depth, 1, n))


def _wkv_part(tt, rw_s, lnx_w, lnx_b, o_ref, s_ref):
    n = WKV_CHUNK
    nc = tt // n

    row = lax.broadcasted_iota(jnp.int32, (n, n), 0)
    col = lax.broadcasted_iota(jnp.int32, (n, n), 1)
    ltri = jnp.where(row >= col, 1.0, 0.0).astype(BF16)
    strict = row > col
    incl = row >= col
    eye = jnp.where(row == col, 1.0, 0.0)
    same = lambda s: jnp.right_shift(row, s) == jnp.right_shift(col, s)

    items = [(ci, hd) for ci in range(nc) for hd in range(H_A)]
    head = lambda x, hd: x[:, hd * HEAD_A:(hd + 1) * HEAD_A]

    a_t, r_t, b_t, k_t, b_h, k_h, vv, g_end = [], [], [], [], [], [], [], []
    for ci in range(nc):
        rows = slice(ci * n, (ci + 1) * n)
        r = rw_s[rows, 0 * D_MIX:1 * D_MIX]
        lw = rw_s[rows, 1 * D_MIX:2 * D_MIX]
        k = rw_s[rows, 2 * D_MIX:3 * D_MIX]
        kk = rw_s[rows, 4 * D_MIX:5 * D_MIX]
        b = rw_s[rows, 5 * D_MIX:6 * D_MIX]
        e = _dot_sel(ltri, lw)
        eg = jnp.exp(e)
        ig = jnp.exp(-e)
        ge = eg[n - 1:n, :]
        a_t.append((-kk * jnp.exp(e - lw)).astype(BF16))
        r_t.append(r * eg)
        b_t.append((b * ig).astype(BF16))
        k_t.append((k * ig).astype(BF16))
        b_h.append((b * ig * ge).astype(BF16))
        k_h.append((k * ig * ge).astype(BF16))
        vv.append(rw_s[rows, 3 * D_MIX:4 * D_MIX].astype(BF16))
        g_end.append(ge)

    gm = [_dot_nt(jnp.concatenate([head(a_t[ci], hd), head(r_t[ci], hd).astype(BF16)], axis=0),
                  jnp.concatenate([head(b_t[ci], hd), head(k_t[ci], hd)], axis=0)) for ci, hd in items]
    a_ab = [jnp.where(strict, m[:n, :n], 0.0) for m in gm]
    a_ak = [jnp.where(strict, m[:n, n:], 0.0) for m in gm]
    a_rb = [jnp.where(incl, m[n:, :n], 0.0) for m in gm]
    a_rk = [jnp.where(incl, m[n:, n:], 0.0) for m in gm]
    inv = [eye + jnp.where(same(1), a, 0.0) for a in a_ab]
    for s in range(2, int(math.log2(n)) + 1):
        off_mask = same(s) & jnp.logical_not(same(s - 1))
        tmp = [_dot(t, jnp.where(off_mask, a, 0.0)) for t, a in zip(inv, a_ab)]
        inv = [t + _dot(x, t) for t, x in zip(inv, tmp)]
    akv = [_dot(a_ak[i], head(vv[ci], hd)) for i, (ci, hd) in enumerate(items)]
    pq = [_dot(inv[i], jnp.concatenate([head(a_t[ci], hd), akv[i].astype(BF16)], axis=1))
          for i, (ci, hd) in enumerate(items)]
    mn = [_dot_tn(pq[i], head(b_h[ci], hd)) for i, (ci, hd) in enumerate(items)]
    kv = [_dot_tn(head(vv[ci], hd), head(k_h[ci], hd)) for ci, hd in items]
    rq = [_dot(a_rb[i], pq[i]) for i in range(len(items))]
    ark = [_dot(a_rk[i], head(vv[ci], hd)) for i, (ci, hd) in enumerate(items)]

    state = [s_ref[hd] for hd in range(H_A)]
    for ci in range(nc):
        outs = []
        for hd in range(H_A):
            i = ci * H_A + hd
            s0 = state[hd]
            ro = head(r_t[ci], hd) + rq[i][:, :n]
            o = _dot_nt(ro, s0) + rq[i][:, n:] + ark[i]
            state[hd] = s0 * head(g_end[ci], hd) + _dot(s0, mn[i][:n, :]) + (mn[i][n:, :] + kv[i])
            mu = jnp.mean(o, axis=-1, keepdims=True)
            var = jnp.mean(jnp.square(o - mu), axis=-1, keepdims=True)
            outs.append((o - mu) * lax.rsqrt(var + LNX_EPS))
        rows = slice(ci * n, (ci + 1) * n)
        on = jnp.concatenate(outs, axis=1) * lnx_w + lnx_b
        g = rw_s[rows, 6 * D_MIX:7 * D_MIX]
        bonus = rw_s[rows, 7 * D_MIX:8 * D_MIX]
        o_ref[0, rows, :] = ((on + bonus) * g).astype(BF16)
    for hd in range(H_A):
        s_ref[hd] = state[hd]


def _s5_part(u, perm, bblk, cre_ref, cim_ref, ab_re, ab_im, d_row, bu_s, x_s, carry_s):
    half = S5_STATES
    steps = S5_TILE // S5_SUBSEQ

    u_p = _dot_sel(perm, u)
    bu_s[...] = jnp.dot(u_p.astype(BF16), bblk, preferred_element_type=F32)

    a_re = jnp.broadcast_to(ab_re, (S5_SUBSEQ, half))
    a_im = jnp.broadcast_to(ab_im, (S5_SUBSEQ, half))

    def step(i, st):
        s_re, s_im = st
        rows = slice(i * S5_SUBSEQ, (i + 1) * S5_SUBSEQ)
        n_re = a_re * s_re - a_im * s_im + bu_s[rows, 0:half]
        n_im = a_re * s_im + a_im * s_re + bu_s[rows, half:]
        return n_re, n_im

    zero = jnp.zeros((S5_SUBSEQ, half), F32)
    st = (zero, zero)
    for i in range(steps):
        st = step(i, st)
    e_re, e_im = st

    p_re, p_im = ab_re, ab_im
    for _ in range(int(math.log2(steps))):
        p_re, p_im = p_re * p_re - p_im * p_im, 2.0 * p_re * p_im

    c_re, c_im = carry_s[0:1, :], carry_s[1:2, :]
    in_re, in_im = [], []
    for j in range(S5_SUBSEQ):
        in_re.append(c_re)
        in_im.append(c_im)
        c_re, c_im = (p_re * c_re - p_im * c_im + e_re[j:j + 1, :],
                      p_re * c_im + p_im * c_re + e_im[j:j + 1, :])
    carry_s[0:1, :] = c_re
    carry_s[1:2, :] = c_im

    st = (jnp.concatenate(in_re, axis=0), jnp.concatenate(in_im, axis=0))
    for i in range(steps):
        st = step(i, st)
        rows = slice(i * S5_SUBSEQ, (i + 1) * S5_SUBSEQ)
        x_s[rows, 0:half] = st[0]
        x_s[rows, half:] = st[1]

    y = (jnp.dot(x_s[:, 0:half].astype(BF16), cre_ref[0], preferred_element_type=F32)
         - jnp.dot(x_s[:, half:].astype(BF16), cim_ref[0], preferred_element_type=F32))
    f_p = _gelu_tanh(y + d_row * u_p).astype(BF16)
    return lax.dot_general(perm, f_p, (((0,), (0,)), ((), ())), preferred_element_type=F32).astype(BF16)


def _mix_kernel(l, tt, *refs):
    has_vmix = l > 0
    refs = list(refs)
    (x_ref, mod_ref, ng_ref, win_ref, mu_ref, rows_ref, w2_ref, a2_ref, g2_ref, seg_ref, ws_ref, sgb_ref,
     cw_ref, perm_ref, bblk_ref, cre_ref, cim_ref, ab_ref) = refs[:18]
    refs = refs[18:]
    if has_vmix:
        vf_ref, v0_ref, v1_ref, v2_ref = refs[:4]
        refs = refs[4:]
        fa_ref, fb_ref, fc_ref, fd_ref = refs[:4]
        refs = refs[4:]
    else:
        fa_ref, fb_ref, fc_ref, fd_ref, vout_ref = refs[:5]
        refs = refs[5:]
    pa_s, z_s, rw_s, s_ref, bu_s, x_s, carry_s = refs
    prow = lambda i: rows_ref[i, l:l + 1, :]

    @pl.when(pl.program_id(1) == 0)
    def _():
        pa_s[0:8, :] = jnp.zeros((8, A_COLS), F32)
        z_s[0:8, :] = jnp.zeros((8, D_MIX), F32)
        s_ref[...] = jnp.zeros(s_ref.shape, F32)
        carry_s[...] = jnp.zeros(carry_s.shape, F32)

    mod = mod_ref[0, 0]
    h = _rms_mod(x_ref[0], ng_ref[l:l + 1, :], mod[1:2, :], mod[0:1, :])
    p = jnp.dot(h.astype(BF16), win_ref[0], preferred_element_type=F32)

    pa_s[8:8 + tt, :] = p[:, :A_COLS]
    pa = p[:, :A_COLS]
    prev = pa_s[7:7 + tt, :]
    pa = pa + (prev - pa) * mu_ref[l:l + 1, :]
    pa_s[0:8, :] = pa_s[tt:tt + 8, :]
    r = pa[:, 0:D_MIX]
    k = pa[:, D_MIX:2 * D_MIX]
    v = pa[:, 2 * D_MIX:3 * D_MIX]
    lora = pa[:, 3 * D_MIX:A_COLS]

    def lora_w(w_ref, start):
        w = w_ref[0]
        parts = []
        if start:
            parts.append(jnp.zeros((start, D_MIX), F32))
        parts.append(w)
        if LORA_COLS - start - w.shape[0]:
            parts.append(jnp.zeros((LORA_COLS - start - w.shape[0], D_MIX), F32))
        return jnp.concatenate(parts, axis=0).astype(BF16)

    lw = -math.exp(-0.5) * _sigmoid(prow(ROW_W0) + _dot(jnp.tanh(lora), lora_w(w2_ref, 0)))
    if has_vmix:
        vgate = _sigmoid(v0_ref[l - 1:l, :] + _dot(_dot(v, v1_ref[0]), v2_ref[0]))
        v = v + (vf_ref[0] - v) * vgate
    else:
        vout_ref[0] = v
    a = _sigmoid(prow(ROW_A0) + _dot(lora, lora_w(a2_ref, LORA_W)))
    g = _dot(_sigmoid(lora), lora_w(g2_ref, LORA_W + LORA_A))
    kk = k * prow(ROW_KK)
    seg = seg_ref[...]
    kk_norm = jnp.sqrt(_dot_x_sel(kk * kk, seg))
    kk = kk / jnp.maximum(kk_norm, 1e-12)
    k = k * (1.0 + (a - 1.0) * prow(ROW_KA))
    bonus = _dot_x_sel(r * k * prow(ROW_RK), seg) * v
    rw_s[:, 0 * D_MIX:1 * D_MIX] = r
    rw_s[:, 1 * D_MIX:2 * D_MIX] = lw
    rw_s[:, 2 * D_MIX:3 * D_MIX] = k
    rw_s[:, 3 * D_MIX:4 * D_MIX] = v
    rw_s[:, 4 * D_MIX:5 * D_MIX] = kk
    rw_s[:, 5 * D_MIX:6 * D_MIX] = kk * a
    rw_s[:, 6 * D_MIX:7 * D_MIX] = g
    rw_s[:, 7 * D_MIX:8 * D_MIX] = bonus

    z = _gelu_tanh(p[:, OFF_B:OFF_C])
    su = z[:, :D_MIX]
    sv = z[:, D_MIX:]
    mu_v = jnp.mean(sv, axis=-1, keepdims=True)
    var_v = jnp.mean(jnp.square(sv - mu_v), axis=-1, keepdims=True)
    sv = (sv - mu_v) * lax.rsqrt(var_v + LN_EPS) * prow(ROW_SGW) + prow(ROW_SGB)
    row = lax.broadcasted_iota(jnp.int32, (CHUNK, CHUNK), 0)
    col = lax.broadcasted_iota(jnp.int32, (CHUNK, CHUNK), 1)
    causal = row >= col
    wsm = [jnp.where(causal, ws_ref[0, gi], 0.0).astype(BF16) for gi in range(G_B)]
    sv_b = sv.astype(BF16)
    for n in range(tt // CHUNK):
        rows = slice(n * CHUNK, (n + 1) * CHUNK)
        mixed = jnp.concatenate(
            [jnp.dot(wsm[gi], sv_b[rows, gi * GROUP_B:(gi + 1) * GROUP_B], preferred_element_type=F32)
             for gi in range(G_B)], axis=1) + sgb_ref[0]
        fb_ref[0, rows, :] = (su[rows, :] * mixed).astype(BF16)

    pc = p[:, OFF_C:OFF_D]
    bg = pc[:, :D_MIX]
    zc = pc[:, D_MIX:2 * D_MIX] * pc[:, 2 * D_MIX:]
    z_s[8:8 + tt, :] = zc
    y = (cw_ref[l, 0:1, :] * z_s[6:6 + tt, :] + cw_ref[l, 1:2, :] * z_s[7:7 + tt, :]
         + cw_ref[l, 2:3, :] * zc)
    z_s[0:8, :] = z_s[tt:tt + 8, :]
    fc_ref[0] = (bg * y).astype(BF16)

    u5 = p[:, OFF_D:OFF_G]

    _wkv_part(tt, rw_s, prow(ROW_LNXW), prow(ROW_LNXB), fa_ref, s_ref)

    for n in range(tt // S5_TILE):
        rows = slice(n * S5_TILE, (n + 1) * S5_TILE)
        fd_ref[0, rows, :] = _s5_part(u5[rows, :], perm_ref[...], bblk_ref[0], cre_ref, cim_ref,
                                      ab_ref[0, 0:1, :], ab_ref[0, 1:2, :], prow(ROW_S5D),
                                      bu_s, x_s, carry_s)


def _mix_call(l, x, mod4, ng, win, mu, rows, w2, a2, g2, seg, ws, sgb, cw, perm, bblk, cre, cim, ab,
              v0, v1, v2, v_first, tt):
    bsz, seq, d = x.shape
    has_vmix = l > 0
    ins = [x, mod4, ng, win, mu, rows, w2, a2, g2, seg, ws, sgb, cw, perm, bblk, cre, cim, ab]
    in_specs = [
        _tok_spec(tt, d), _mod_spec(mod4, l), _const_spec(ng),
        pl.BlockSpec((1, d, OFF_G), lambda b, t: (l, 0, 0)),
        _const_spec(mu), _const_spec(rows), _layer_spec(w2, l), _layer_spec(a2, l), _layer_spec(g2, l),
        _const_spec(seg), _layer_spec(ws, l), _layer_spec(sgb, l), _const_spec(cw), _const_spec(perm),
        _layer_spec(bblk, l), _layer_spec(cre, l), _layer_spec(cim, l), _layer_spec(ab, l),
    ]
    feat = jax.ShapeDtypeStruct((bsz, seq, D_MIX), BF16)
    out_shape = [feat, feat, feat, feat]
    out_specs = [_tok_spec(tt, D_MIX)] * 4
    if has_vmix:
        ins += [v_first, v0, v1, v2]
        in_specs += [_tok_spec(tt, D_MIX), _const_spec(v0), _layer_spec(v1, l - 1), _layer_spec(v2, l - 1)]
    else:
        out_shape.append(jax.ShapeDtypeStruct((bsz, seq, D_MIX), F32))
        out_specs.append(_tok_spec(tt, D_MIX))
    return pl.pallas_call(
        functools.partial(_mix_kernel, l, tt),
        grid=(bsz, seq // tt),
        in_specs=in_specs,
        out_specs=out_specs,
        out_shape=out_shape,
        scratch_shapes=[pltpu.VMEM((tt + 8, A_COLS), F32), pltpu.VMEM((tt + 8, D_MIX), F32),
                        pltpu.VMEM((tt, RW_COLS), F32), pltpu.VMEM((H_A, HEAD_A, HEAD_A), F32),
                        pltpu.VMEM((S5_TILE, S5_LANES), F32), pltpu.VMEM((S5_TILE, S5_LANES), F32),
                        pltpu.VMEM((8, S5_STATES), F32)],
        compiler_params=_cparams(("parallel", "arbitrary")),
        name="mix",
    )(*ins)


def _s5_param_kernel(are_ref, aim_ref, ldt_ref, bre_ref, bim_ref, abre_ref, abim_ref, bbre_ref, bbim_ref):
    lam_re = jnp.minimum(are_ref[...], -1e-4)
    lam_im = aim_ref[...]
    dt = jnp.exp(ldt_ref[...])
    mag = jnp.exp(lam_re * dt)
    ab_re = mag * jnp.cos(lam_im * dt)
    ab_im = mag * jnp.sin(lam_im * dt)
    den = lam_re * lam_re + lam_im * lam_im
    q_re = ((ab_re - 1.0) * lam_re + ab_im * lam_im) / den
    q_im = (ab_im * lam_re - (ab_re - 1.0) * lam_im) / den
    abre_ref[...] = ab_re
    abim_ref[...] = ab_im
    b_re = bre_ref[...]
    b_im = bim_ref[...]
    bbre_ref[...] = q_re * b_re - q_im * b_im
    bbim_ref[...] = q_re * b_im + q_im * b_re


def _s5_param_call(a_re, a_im, log_dt, b_re, b_im):
    rows = a_re.size
    col = lambda a: a.reshape(rows, 1).astype(F32)
    ldt = jnp.repeat(log_dt.astype(F32), N_STATE, axis=-1).reshape(rows, 1)
    return pl.pallas_call(
        _s5_param_kernel,
        out_shape=(jax.ShapeDtypeStruct((rows, 1), F32), jax.ShapeDtypeStruct((rows, 1), F32),
                   jax.ShapeDtypeStruct((rows, GROUP_D), F32), jax.ShapeDtypeStruct((rows, GROUP_D), F32)),
        name="s5_params",
    )(col(a_re), col(a_im), ldt, b_re.reshape(rows, GROUP_D).astype(F32),
      b_im.reshape(rows, GROUP_D).astype(F32))


def _merge_kernel(l, x_ref, mod_ref, ng_ref, fa_ref, fb_ref, fc_ref, fd_ref, wg_ref, wa_ref, wb_ref, wc_ref,
                  glu_ref, wo_ref, o_ref):
    d = D_MODEL
    mod = mod_ref[0, 0]
    x = x_ref[0]
    hb = _rms_mod(x, ng_ref[l:l + 1, :], mod[1:2, :], mod[0:1, :]).astype(BF16)

    def gate(i):
        return _sigmoid(jnp.dot(hb, wg_ref[0, :, i * d:(i + 1) * d], preferred_element_type=F32))

    merged = gate(0) * jnp.dot(fa_ref[0], wa_ref[0], preferred_element_type=F32)
    merged += gate(1) * jnp.dot(fb_ref[0], wb_ref[0], preferred_element_type=F32)
    merged += gate(2) * jnp.dot(fc_ref[0], wc_ref[0], preferred_element_type=F32)
    hd = jnp.dot(fd_ref[0], glu_ref[0], preferred_element_type=F32)
    merged += gate(3) * (hd[:, :d] * _sigmoid(hd[:, d:]))
    o_ref[0] = x + mod[2:3, :] * jnp.dot(merged.astype(BF16), wo_ref[0], preferred_element_type=F32)


def _merge_call(l, x, mod4, ng, fa, fb, fc, fd, wg, wa, wb, wc, glu, wo, tm):
    bsz, seq, d = x.shape
    feat = _tok_spec(tm, D_MIX)
    return pl.pallas_call(
        functools.partial(_merge_kernel, l),
        grid=(bsz, seq // tm),
        in_specs=[_tok_spec(tm, d), _mod_spec(mod4, l), _const_spec(ng), feat, feat, feat, feat,
                  _layer_spec(wg, l), _layer_spec(wa, l), _layer_spec(wb, l), _layer_spec(wc, l),
                  _layer_spec(glu, l), _layer_spec(wo, l)],
        out_specs=_tok_spec(tm, d),
        out_shape=jax.ShapeDtypeStruct((bsz, seq, d), F32),
        compiler_params=_cparams(("parallel", "parallel")),
        name="merge",
    )(x, mod4, ng, fa, fb, fc, fd, wg, wa, wb, wc, glu, wo)


def _ffn_kernel(l, final, x_ref, mod_ref, ng_ref, w1_ref, w2_ref, fg_ref, o_ref):
    mod = mod_ref[0, 0]
    x = x_ref[0]
    hb = _rms_mod(x, ng_ref[l:l + 1, :], mod[4:5, :], mod[3:4, :]).astype(BF16)
    acc = jnp.zeros(x.shape, F32)
    step = D_MODEL
    for j in range(D_FF // step):
        a = jnp.dot(hb, w1_ref[0, :, j * step:(j + 1) * step], preferred_element_type=F32)
        a = jnp.square(jnp.maximum(a, 0.0))
        acc += jnp.dot(a.astype(BF16), w2_ref[0, j * step:(j + 1) * step, :], preferred_element_type=F32)
    y = x + mod[5:6, :] * acc
    if final:
        ms = jnp.mean(y * y, axis=-1, keepdims=True)
        y = y * lax.rsqrt(ms + EPS) * fg_ref[...]
    o_ref[0] = y


def _ffn_call(l, x, mod4, ng, w1, w2, fg, final, tm):
    bsz, seq, d = x.shape
    return pl.pallas_call(
        functools.partial(_ffn_kernel, l, final),
        grid=(bsz, seq // tm),
        in_specs=[_tok_spec(tm, d), _mod_spec(mod4, l), _const_spec(ng), _layer_spec(w1, l),
                  _layer_spec(w2, l), _const_spec(fg)],
        out_specs=_tok_spec(tm, d),
        out_shape=jax.ShapeDtypeStruct((bsz, seq, d), F32),
        compiler_params=_cparams(("parallel", "parallel")),
        name="ffn",
    )(x, mod4, ng, w1, w2, fg)


def _tile(seq, want):
    return want if seq % want == 0 else seq


def kernel(x, c, ada_w, ada_b, norm_mix_g, w_in, rwkv_mu, rwkv_w0, rwkv_w2, rwkv_a0, rwkv_a2, rwkv_g2,
           rwkv_v0, rwkv_v1, rwkv_v2, rwkv_kk, rwkv_ka, rwkv_rk, rwkv_lnx_w, rwkv_lnx_b, rwkv_out,
           sg_ln_w, sg_ln_b, sg_ws, sg_bs, sg_out, conv_w, conv_out, s5_a_re, s5_a_im, s5_b_re, s5_b_im,
           s5_c_re, s5_c_im, s5_d, s5_log_dt, s5_glu_w, w_o, norm_ffn_g, ffn_w1, ffn_w2, final_g):
    in_dtype = x.dtype
    bsz, seq, d = x.shape
    depth = ada_w.shape[0]
    x = x.astype(F32)

    tt = _tile(seq, 512)
    tm = _tile(seq, 512)

    c_rows = 16
    c_pad = jnp.pad(c.astype(F32), ((0, c_rows - bsz), (0, 0)))
    mod4 = _ada_call(c_pad, ada_w.astype(F32), ada_b.astype(F32)).reshape(depth, c_rows, 6, d)

    rows = jnp.stack([rwkv_w0, rwkv_a0, rwkv_kk, rwkv_ka, rwkv_rk.reshape(depth, D_MIX), rwkv_lnx_w,
                      rwkv_lnx_b, sg_ln_w, sg_ln_b, s5_d], axis=0).astype(F32)
    sgb = jnp.repeat(jnp.swapaxes(sg_bs, 1, 2), GROUP_B, axis=2).astype(F32)
    win = w_in.astype(BF16)
    wg = win[:, :, OFF_G:]
    wa, wb, wc = rwkv_out.astype(BF16), sg_out.astype(BF16), conv_out.astype(BF16)
    glu = s5_glu_w.astype(BF16)
    wo = w_o.astype(BF16)
    w1 = ffn_w1.astype(BF16)
    w2 = ffn_w2.astype(BF16)
    fg = final_g.reshape(1, d).astype(F32)

    head_id = np.arange(D_MIX) // HEAD_A
    seg = jnp.asarray(head_id[:, None] == head_id[None, :], BF16)
    steps = S5_TILE // S5_SUBSEQ
    dst = np.arange(S5_TILE)
    src = (dst % S5_SUBSEQ) * steps + dst // S5_SUBSEQ
    perm = jnp.asarray(src[:, None] == np.arange(S5_TILE)[None, :], BF16)

    abre, abim, bbre, bbim = _s5_param_call(s5_a_re, s5_a_im, s5_log_dt, s5_b_re, s5_b_im)
    ab = jnp.concatenate([abre.reshape(depth, 1, S5_STATES), abim.reshape(depth, 1, S5_STATES)], axis=1)
    eye_g = jnp.asarray(np.eye(G_D), F32)

    def in_blk(bb):
        t = bb.reshape(depth, G_D, N_STATE, GROUP_D)
        return jnp.einsum('lgnc,gh->lgchn', t, eye_g).reshape(depth, D_MIX, S5_STATES)

    def out_blk(cc):
        return jnp.einsum('lgcn,gh->lgnhc', cc.astype(F32), eye_g).reshape(depth, S5_STATES, D_MIX).astype(BF16)

    bblk = jnp.concatenate([in_blk(bbre), in_blk(bbim)], axis=2).astype(BF16)
    cre = out_blk(s5_c_re)
    cim = out_blk(s5_c_im)

    f32 = lambda a: a.astype(F32)
    v_first = None
    for l in range(depth):
        outs = _mix_call(l, x, mod4, f32(norm_mix_g), win, f32(rwkv_mu), rows, f32(rwkv_w2), f32(rwkv_a2),
                         f32(rwkv_g2), seg, f32(sg_ws), sgb, f32(conv_w), perm, bblk, cre, cim, ab,
                         f32(rwkv_v0), f32(rwkv_v1), f32(rwkv_v2), v_first, tt)
        fa, fb, fc, fd = outs[:4]
        if l == 0:
            v_first = outs[4]
        x = _merge_call(l, x, mod4, f32(norm_mix_g), fa, fb, fc, fd, wg, wa, wb, wc, glu, wo, tm)
        x = _ffn_call(l, x, mod4, f32(norm_ffn_g), w1, w2, fg, l == depth - 1, tm)
    return x.astype(in_dtype)
```

```python
import functools
import math

import numpy as np
import jax
import jax.numpy as jnp
from jax import lax
from jax.experimental import pallas as pl
from jax.experimental.pallas import tpu as pltpu

F32 = jnp.float32
BF16 = jnp.bfloat16

D_MODEL = 1024
N_BRANCH = 4
D_MIX = D_MODEL // N_BRANCH
HEAD_A = 64
H_A = D_MIX // HEAD_A
LORA_W = 32
LORA_A = 32
LORA_G = 64
LNX_EPS = 64e-5
CHUNK = 128
GROUP_B = 64
G_B = D_MIX // GROUP_B
CONV_K = 3
GROUP_D = 16
G_D = D_MIX // GROUP_D
N_STATE = 64
D_FF = 4 * D_MODEL
EPS = 1e-6
LN_EPS = 1e-5

A_COLS = 3 * D_MIX + LORA_W + LORA_A + LORA_G
B_COLS = 2 * D_MIX
C_COLS = 3 * D_MIX
D_COLS = D_MIX
OFF_B = A_COLS
OFF_C = OFF_B + B_COLS
OFF_D = OFF_C + C_COLS
OFF_G = OFF_D + D_COLS
LORA_COLS = LORA_W + LORA_A + LORA_G

WKV_CHUNK = 64
S5_STATES = G_D * N_STATE
S5_LANES = 2 * S5_STATES
S5_SUBSEQ = 8
S5_TILE = 256
RW_COLS = 8 * D_MIX

(ROW_W0, ROW_A0, ROW_KK, ROW_KA, ROW_RK, ROW_LNXW, ROW_LNXB, ROW_SGW, ROW_SGB, ROW_S5D) = range(10)

VMEM_LIMIT = 56 * 1024 * 1024


def _cparams(sem):
    return pltpu.CompilerParams(dimension_semantics=sem, vmem_limit_bytes=VMEM_LIMIT)


def _dot(a, b):
    return jnp.dot(a.astype(BF16), b.astype(BF16), preferred_element_type=F32)


def _dot_nt(a, b):
    return lax.dot_general(a.astype(BF16), b.astype(BF16), (((1,), (1,)), ((), ())),
                           preferred_element_type=F32)


def _dot_tn(a, b):
    return lax.dot_general(a.astype(BF16), b.astype(BF16), (((0,), (0,)), ((), ())),
                           preferred_element_type=F32)


def _split3(x):
    hi = x.astype(BF16)
    r1 = x - hi.astype(F32)
    mid = r1.astype(BF16)
    lo = (r1 - mid.astype(F32)).astype(BF16)
    return hi, mid, lo


def _dot_sel(sel, x):
    hi, mid, lo = _split3(x)
    return (jnp.dot(sel, hi, preferred_element_type=F32) + jnp.dot(sel, mid, preferred_element_type=F32)
            + jnp.dot(sel, lo, preferred_element_type=F32))


def _dot_x_sel(x, sel):
    hi, mid, lo = _split3(x)
    return (jnp.dot(hi, sel, preferred_element_type=F32) + jnp.dot(mid, sel, preferred_element_type=F32)
            + jnp.dot(lo, sel, preferred_element_type=F32))


def _dot3(a, b):
    ah, am, _ = _split3(a)
    bh, bm, _ = _split3(b)
    return (jnp.dot(ah, bh, preferred_element_type=F32) + jnp.dot(ah, bm, preferred_element_type=F32)
            + jnp.dot(am, bh, preferred_element_type=F32))


def _sigmoid(x):
    return 0.5 * jnp.tanh(0.5 * x) + 0.5


def _gelu_tanh(x):
    return 0.5 * x * (1.0 + jnp.tanh(math.sqrt(2.0 / math.pi) * (x + 0.044715 * (x * x * x))))


def _rms_mod(x, g, sc, sh):
    ms = jnp.mean(x * x, axis=-1, keepdims=True)
    return x * lax.rsqrt(ms + EPS) * g * (1.0 + sc) + sh


def _layer_spec(a, l):
    return pl.BlockSpec((1,) + a.shape[1:], lambda b, t: (l,) + (0,) * (a.ndim - 1))


def _weight_spec(a, l):
    return pl.BlockSpec((1,) + a.shape[1:], lambda b, t: (l,) + (0,) * (a.ndim - 1),
                        pipeline_mode=pl.Buffered(1))


def _mod_spec(mod4, l):
    return pl.BlockSpec((1, 1) + mod4.shape[2:], lambda b, t: (l, b, 0, 0))


def _const_spec(a):
    return pl.BlockSpec(a.shape, lambda b, t: (0,) * a.ndim)


def _tok_spec(tt, cols):
    return pl.BlockSpec((1, tt, cols), lambda b, t: (b, t, 0))


def _ada_kernel(c_ref, w_ref, b_ref, o_ref):
    c = c_ref[...]
    ca = c * _sigmoid(c)
    o_ref[0] = _dot3(ca, w_ref[0]) + b_ref[0]


def _ada_call(c_pad, ada_w, ada_b):
    depth, d, n = ada_w.shape
    rows = c_pad.shape[0]
    bn = 1536
    return pl.pallas_call(
        _ada_kernel,
        grid=(depth, n // bn),
        in_specs=[
            pl.BlockSpec((rows, d), lambda l, j: (0, 0)),
            pl.BlockSpec((1, d, bn), lambda l, j: (l, 0, j)),
            pl.BlockSpec((1, 1, bn), lambda l, j: (l, 0, j)),
        ],
        out_specs=pl.BlockSpec((1, rows, bn), lambda l, j: (l, 0, j)),
        out_shape=jax.ShapeDtypeStruct((depth, rows, n), F32),
        compiler_params=_cparams(("parallel", "parallel")),
        name="ada_mod",
    )(c_pad, ada_w, ada_b.reshape(depth, 1, n))


def _wkv_part(tt, rw_s, lnx_w, lnx_b, o_ref, s_ref):
    n = WKV_CHUNK
    nc = tt // n

    row = lax.broadcasted_iota(jnp.int32, (n, n), 0)
    col = lax.broadcasted_iota(jnp.int32, (n, n), 1)
    ltri = jnp.where(row >= col, 1.0, 0.0).astype(BF16)
    strict = row > col
    incl = row >= col
    eye = jnp.where(row == col, 1.0, 0.0)
    same = lambda s: jnp.right_shift(row, s) == jnp.right_shift(col, s)

    items = [(ci, hd) for ci in range(nc) for hd in range(H_A)]
    head = lambda x, hd: x[:, hd * HEAD_A:(hd + 1) * HEAD_A]

    a_t, r_t, b_t, k_t, b_h, k_h, vv, g_end = [], [], [], [], [], [], [], []
    for ci in range(nc):
        rows = slice(ci * n, (ci + 1) * n)
        r = rw_s[rows, 0 * D_MIX:1 * D_MIX]
        lw = rw_s[rows, 1 * D_MIX:2 * D_MIX]
        k = rw_s[rows, 2 * D_MIX:3 * D_MIX]
        kk = rw_s[rows, 4 * D_MIX:5 * D_MIX]
        b = rw_s[rows, 5 * D_MIX:6 * D_MIX]
        e = _dot_sel(ltri, lw)
        eg = jnp.exp(e)
        ig = jnp.exp(-e)
        ge = eg[n - 1:n, :]
        a_t.append((-kk * jnp.exp(e - lw)).astype(BF16))
        r_t.append(r * eg)
        b_t.append((b * ig).astype(BF16))
        k_t.append((k * ig).astype(BF16))
        b_h.append((b * ig * ge).astype(BF16))
        k_h.append((k * ig * ge).astype(BF16))
        vv.append(rw_s[rows, 3 * D_MIX:4 * D_MIX].astype(BF16))
        g_end.append(ge)

    gm = [_dot_nt(jnp.concatenate([head(a_t[ci], hd), head(r_t[ci], hd).astype(BF16)], axis=0),
                  jnp.concatenate([head(b_t[ci], hd), head(k_t[ci], hd)], axis=0)) for ci, hd in items]
    a_ab = [jnp.where(strict, m[:n, :n], 0.0) for m in gm]
    a_ak = [jnp.where(strict, m[:n, n:], 0.0) for m in gm]
    a_rb = [jnp.where(incl, m[n:, :n], 0.0) for m in gm]
    a_rk = [jnp.where(incl, m[n:, n:], 0.0) for m in gm]
    inv = [eye + jnp.where(same(1), a, 0.0) for a in a_ab]
    for s in range(2, int(math.log2(n)) + 1):
        off_mask = same(s) & jnp.logical_not(same(s - 1))
        tmp = [_dot(t, jnp.where(off_mask, a, 0.0)) for t, a in zip(inv, a_ab)]
        inv = [t + _dot(x, t) for t, x in zip(inv, tmp)]
    akv = [_dot(a_ak[i], head(vv[ci], hd)) for i, (ci, hd) in enumerate(items)]
    pq = [_dot(inv[i], jnp.concatenate([head(a_t[ci], hd), akv[i].astype(BF16)], axis=1))
          for i, (ci, hd) in enumerate(items)]
    mn = [_dot_tn(pq[i], head(b_h[ci], hd)) for i, (ci, hd) in enumerate(items)]
    kv = [_dot_tn(head(vv[ci], hd), head(k_h[ci], hd)) for ci, hd in items]
    rq = [_dot(a_rb[i], pq[i]) for i in range(len(items))]
    ark = [_dot(a_rk[i], head(vv[ci], hd)) for i, (ci, hd) in enumerate(items)]

    state = [s_ref[hd] for hd in range(H_A)]
    for ci in range(nc):
        outs = []
        for hd in range(H_A):
            i = ci * H_A + hd
            s0 = state[hd]
            ro = head(r_t[ci], hd) + rq[i][:, :n]
            o = _dot_nt(ro, s0) + rq[i][:, n:] + ark[i]
            state[hd] = s0 * head(g_end[ci], hd) + _dot(s0, mn[i][:n, :]) + (mn[i][n:, :] + kv[i])
            mu = jnp.mean(o, axis=-1, keepdims=True)
            var = jnp.mean(jnp.square(o - mu), axis=-1, keepdims=True)
            outs.append((o - mu) * lax.rsqrt(var + LNX_EPS))
        rows = slice(ci * n, (ci + 1) * n)
        on = jnp.concatenate(outs, axis=1) * lnx_w + lnx_b
        g = rw_s[rows, 6 * D_MIX:7 * D_MIX]
        bonus = rw_s[rows, 7 * D_MIX:8 * D_MIX]
        o_ref[0, rows, :] = ((on + bonus) * g).astype(BF16)
    for hd in range(H_A):
        s_ref[hd] = state[hd]


def _s5_part(u, perm, bblk, cre_ref, cim_ref, ab_re, ab_im, d_row, bu_s, x_s, carry_s):
    half = S5_STATES
    steps = S5_TILE // S5_SUBSEQ

    u_p = _dot_sel(perm, u)
    bu_s[...] = jnp.dot(u_p.astype(BF16), bblk, preferred_element_type=F32)

    a_re = jnp.broadcast_to(ab_re, (S5_SUBSEQ, half))
    a_im = jnp.broadcast_to(ab_im, (S5_SUBSEQ, half))

    def step(i, st):
        s_re, s_im = st
        rows = slice(i * S5_SUBSEQ, (i + 1) * S5_SUBSEQ)
        n_re = a_re * s_re - a_im * s_im + bu_s[rows, 0:half]
        n_im = a_re * s_im + a_im * s_re + bu_s[rows, half:]
        return n_re, n_im

    zero = jnp.zeros((S5_SUBSEQ, half), F32)
    st = (zero, zero)
    for i in range(steps):
        st = step(i, st)
    e_re, e_im = st

    p_re, p_im = ab_re, ab_im
    for _ in range(int(math.log2(steps))):
        p_re, p_im = p_re * p_re - p_im * p_im, 2.0 * p_re * p_im

    c_re, c_im = carry_s[0:1, :], carry_s[1:2, :]
    in_re, in_im = [], []
    for j in range(S5_SUBSEQ):
        in_re.append(c_re)
        in_im.append(c_im)
        c_re, c_im = (p_re * c_re - p_im * c_im + e_re[j:j + 1, :],
                      p_re * c_im + p_im * c_re + e_im[j:j + 1, :])
    carry_s[0:1, :] = c_re
    carry_s[1:2, :] = c_im

    st = (jnp.concatenate(in_re, axis=0), jnp.concatenate(in_im, axis=0))
    for i in range(steps):
        st = step(i, st)
        rows = slice(i * S5_SUBSEQ, (i + 1) * S5_SUBSEQ)
        x_s[rows, 0:half] = st[0]
        x_s[rows, half:] = st[1]

    y = (jnp.dot(x_s[:, 0:half].astype(BF16), cre_ref[0], preferred_element_type=F32)
         - jnp.dot(x_s[:, half:].astype(BF16), cim_ref[0], preferred_element_type=F32))
    f_p = _gelu_tanh(y + d_row * u_p).astype(BF16)
    return lax.dot_general(perm, f_p, (((0,), (0,)), ((), ())), preferred_element_type=F32).astype(BF16)


def _mix_kernel(l, tt, *refs):
    has_vmix = l > 0
    refs = list(refs)
    (x_ref, mod_ref, ng_ref, win_ref, mu_ref, rows_ref, w2_ref, a2_ref, g2_ref, seg_ref, ws_ref, sgb_ref,
     cw_ref, perm_ref, bblk_ref, cre_ref, cim_ref, ab_ref) = refs[:18]
    refs = refs[18:]
    if has_vmix:
        vf_ref, v0_ref, v1_ref, v2_ref = refs[:4]
        refs = refs[4:]
        fa_ref, fb_ref, fc_ref, fd_ref = refs[:4]
        refs = refs[4:]
    else:
        fa_ref, fb_ref, fc_ref, fd_ref, vout_ref = refs[:5]
        refs = refs[5:]
    pa_s, z_s, rw_s, s_ref, bu_s, x_s, carry_s = refs
    prow = lambda i: rows_ref[i, l:l + 1, :]

    @pl.when(pl.program_id(1) == 0)
    def _():
        pa_s[0:8, :] = jnp.zeros((8, A_COLS), F32)
        z_s[0:8, :] = jnp.zeros((8, D_MIX), F32)
        s_ref[...] = jnp.zeros(s_ref.shape, F32)
        carry_s[...] = jnp.zeros(carry_s.shape, F32)

    mod = mod_ref[0, 0]
    h = _rms_mod(x_ref[0], ng_ref[l:l + 1, :], mod[1:2, :], mod[0:1, :])
    p = jnp.dot(h.astype(BF16), win_ref[0], preferred_element_type=F32)

    pa_s[8:8 + tt, :] = p[:, :A_COLS]
    pa = p[:, :A_COLS]
    prev = pa_s[7:7 + tt, :]
    pa = pa + (prev - pa) * mu_ref[l:l + 1, :]
    pa_s[0:8, :] = pa_s[tt:tt + 8, :]
    r = pa[:, 0:D_MIX]
    k = pa[:, D_MIX:2 * D_MIX]
    v = pa[:, 2 * D_MIX:3 * D_MIX]
    lora = pa[:, 3 * D_MIX:A_COLS]

    def lora_w(w_ref, start):
        w = w_ref[0]
        parts = []
        if start:
            parts.append(jnp.zeros((start, D_MIX), F32))
        parts.append(w)
        if LORA_COLS - start - w.shape[0]:
            parts.append(jnp.zeros((LORA_COLS - start - w.shape[0], D_MIX), F32))
        return jnp.concatenate(parts, axis=0).astype(BF16)

    lw = -math.exp(-0.5) * _sigmoid(prow(ROW_W0) + _dot(jnp.tanh(lora), lora_w(w2_ref, 0)))
    if has_vmix:
        vgate = _sigmoid(v0_ref[l - 1:l, :] + _dot(_dot(v, v1_ref[0]), v2_ref[0]))
        v = v + (vf_ref[0] - v) * vgate
    else:
        vout_ref[0] = v
    a = _sigmoid(prow(ROW_A0) + _dot(lora, lora_w(a2_ref, LORA_W)))
    g = _dot(_sigmoid(lora), lora_w(g2_ref, LORA_W + LORA_A))
    kk = k * prow(ROW_KK)
    seg = seg_ref[...]
    kk_norm = jnp.sqrt(_dot_x_sel(kk * kk, seg))
    kk = kk / jnp.maximum(kk_norm, 1e-12)
    k = k * (1.0 + (a - 1.0) * prow(ROW_KA))
    bonus = _dot_x_sel(r * k * prow(ROW_RK), seg) * v
    rw_s[:, 0 * D_MIX:1 * D_MIX] = r
    rw_s[:, 1 * D_MIX:2 * D_MIX] = lw
    rw_s[:, 2 * D_MIX:3 * D_MIX] = k
    rw_s[:, 3 * D_MIX:4 * D_MIX] = v
    rw_s[:, 4 * D_MIX:5 * D_MIX] = kk
    rw_s[:, 5 * D_MIX:6 * D_MIX] = kk * a
    rw_s[:, 6 * D_MIX:7 * D_MIX] = g
    rw_s[:, 7 * D_MIX:8 * D_MIX] = bonus

    z = _gelu_tanh(p[:, OFF_B:OFF_C])
    su = z[:, :D_MIX]
    sv = z[:, D_MIX:]
    mu_v = jnp.mean(sv, axis=-1, keepdims=True)
    var_v = jnp.mean(jnp.square(sv - mu_v), axis=-1, keepdims=True)
    sv = (sv - mu_v) * lax.rsqrt(var_v + LN_EPS) * prow(ROW_SGW) + prow(ROW_SGB)
    row = lax.broadcasted_iota(jnp.int32, (CHUNK, CHUNK), 0)
    col = lax.broadcasted_iota(jnp.int32, (CHUNK, CHUNK), 1)
    causal = row >= col
    wsm = [jnp.where(causal, ws_ref[0, gi], 0.0).astype(BF16) for gi in range(G_B)]
    sv_b = sv.astype(BF16)
    for n in range(tt // CHUNK):
        rows = slice(n * CHUNK, (n + 1) * CHUNK)
        mixed = jnp.concatenate(
            [jnp.dot(wsm[gi], sv_b[rows, gi * GROUP_B:(gi + 1) * GROUP_B], preferred_element_type=F32)
             for gi in range(G_B)], axis=1) + sgb_ref[0]
        fb_ref[0, rows, :] = (su[rows, :] * mixed).astype(BF16)

    pc = p[:, OFF_C:OFF_D]
    bg = pc[:, :D_MIX]
    zc = pc[:, D_MIX:2 * D_MIX] * pc[:, 2 * D_MIX:]
    z_s[8:8 + tt, :] = zc
    y = (cw_ref[l, 0:1, :] * z_s[6:6 + tt, :] + cw_ref[l, 1:2, :] * z_s[7:7 + tt, :]
         + cw_ref[l, 2:3, :] * zc)
    z_s[0:8, :] = z_s[tt:tt + 8, :]
    fc_ref[0] = (bg * y).astype(BF16)

    u5 = p[:, OFF_D:OFF_G]

    _wkv_part(tt, rw_s, prow(ROW_LNXW), prow(ROW_LNXB), fa_ref, s_ref)

    for n in range(tt // S5_TILE):
        rows = slice(n * S5_TILE, (n + 1) * S5_TILE)
        fd_ref[0, rows, :] = _s5_part(u5[rows, :], perm_ref[...], bblk_ref[0], cre_ref, cim_ref,
                                      ab_ref[0, 0:1, :], ab_ref[0, 1:2, :], prow(ROW_S5D),
                                      bu_s, x_s, carry_s)


def _mix_call(l, x, mod4, ng, win, mu, rows, w2, a2, g2, seg, ws, sgb, cw, perm, bblk, cre, cim, ab,
              v0, v1, v2, v_first, tt):
    bsz, seq, d = x.shape
    has_vmix = l > 0
    ins = [x, mod4, ng, win, mu, rows, w2, a2, g2, seg, ws, sgb, cw, perm, bblk, cre, cim, ab]
    in_specs = [
        _tok_spec(tt, d), _mod_spec(mod4, l), _const_spec(ng),
        pl.BlockSpec((1, d, OFF_G), lambda b, t: (l, 0, 0)),
        _const_spec(mu), _const_spec(rows), _layer_spec(w2, l), _layer_spec(a2, l), _layer_spec(g2, l),
        _const_spec(seg), _layer_spec(ws, l), _layer_spec(sgb, l), _const_spec(cw), _const_spec(perm),
        _layer_spec(bblk, l), _layer_spec(cre, l), _layer_spec(cim, l), _layer_spec(ab, l),
    ]
    feat = jax.ShapeDtypeStruct((bsz, seq, D_MIX), BF16)
    out_shape = [feat, feat, feat, feat]
    out_specs = [_tok_spec(tt, D_MIX)] * 4
    if has_vmix:
        ins += [v_first, v0, v1, v2]
        in_specs += [_tok_spec(tt, D_MIX), _const_spec(v0), _layer_spec(v1, l - 1), _layer_spec(v2, l - 1)]
    else:
        out_shape.append(jax.ShapeDtypeStruct((bsz, seq, D_MIX), F32))
        out_specs.append(_tok_spec(tt, D_MIX))
    return pl.pallas_call(
        functools.partial(_mix_kernel, l, tt),
        grid=(bsz, seq // tt),
        in_specs=in_specs,
        out_specs=out_specs,
        out_shape=out_shape,
        scratch_shapes=[pltpu.VMEM((tt + 8, A_COLS), F32), pltpu.VMEM((tt + 8, D_MIX), F32),
                        pltpu.VMEM((tt, RW_COLS), F32), pltpu.VMEM((H_A, HEAD_A, HEAD_A), F32),
                        pltpu.VMEM((S5_TILE, S5_LANES), F32), pltpu.VMEM((S5_TILE, S5_LANES), F32),
                        pltpu.VMEM((8, S5_STATES), F32)],
        compiler_params=_cparams(("parallel", "arbitrary")),
        name="mix",
    )(*ins)


def _s5_param_kernel(are_ref, aim_ref, ldt_ref, bre_ref, bim_ref, abre_ref, abim_ref, bbre_ref, bbim_ref):
    lam_re = jnp.minimum(are_ref[...], -1e-4)
    lam_im = aim_ref[...]
    dt = jnp.exp(ldt_ref[...])
    mag = jnp.exp(lam_re * dt)
    ab_re = mag * jnp.cos(lam_im * dt)
    ab_im = mag * jnp.sin(lam_im * dt)
    den = lam_re * lam_re + lam_im * lam_im
    q_re = ((ab_re - 1.0) * lam_re + ab_im * lam_im) / den
    q_im = (ab_im * lam_re - (ab_re - 1.0) * lam_im) / den
    abre_ref[...] = ab_re
    abim_ref[...] = ab_im
    b_re = bre_ref[...]
    b_im = bim_ref[...]
    bbre_ref[...] = q_re * b_re - q_im * b_im
    bbim_ref[...] = q_re * b_im + q_im * b_re


def _s5_param_call(a_re, a_im, log_dt, b_re, b_im):
    rows = a_re.size
    col = lambda a: a.reshape(rows, 1).astype(F32)
    ldt = jnp.repeat(log_dt.astype(F32), N_STATE, axis=-1).reshape(rows, 1)
    return pl.pallas_call(
        _s5_param_kernel,
        out_shape=(jax.ShapeDtypeStruct((rows, 1), F32), jax.ShapeDtypeStruct((rows, 1), F32),
                   jax.ShapeDtypeStruct((rows, GROUP_D), F32), jax.ShapeDtypeStruct((rows, GROUP_D), F32)),
        name="s5_params",
    )(col(a_re), col(a_im), ldt, b_re.reshape(rows, GROUP_D).astype(F32),
      b_im.reshape(rows, GROUP_D).astype(F32))


def _merge_kernel(l, x_ref, mod_ref, ng_ref, fa_ref, fb_ref, fc_ref, fd_ref, wg_ref, wa_ref, wb_ref, wc_ref,
                  glu_ref, wo_ref, o_ref):
    d = D_MODEL
    mod = mod_ref[0, 0]
    x = x_ref[0]
    hb = _rms_mod(x, ng_ref[l:l + 1, :], mod[1:2, :], mod[0:1, :]).astype(BF16)

    def gate(i):
        cols = slice(OFF_G + i * d, OFF_G + (i + 1) * d)
        return _sigmoid(jnp.dot(hb, wg_ref[0, :, cols], preferred_element_type=F32))

    merged = gate(0) * jnp.dot(fa_ref[0], wa_ref[0], preferred_element_type=F32)
    merged += gate(1) * jnp.dot(fb_ref[0], wb_ref[0], preferred_element_type=F32)
    merged += gate(2) * jnp.dot(fc_ref[0], wc_ref[0], preferred_element_type=F32)
    hd = jnp.dot(fd_ref[0], glu_ref[0], preferred_element_type=F32)
    merged += gate(3) * (hd[:, :d] * _sigmoid(hd[:, d:]))
    o_ref[0] = x + mod[2:3, :] * jnp.dot(merged.astype(BF16), wo_ref[0], preferred_element_type=F32)


def _merge_call(l, x, mod4, ng, fa, fb, fc, fd, wg, wa, wb, wc, glu, wo, tm):
    bsz, seq, d = x.shape
    feat = _tok_spec(tm, D_MIX)
    return pl.pallas_call(
        functools.partial(_merge_kernel, l),
        grid=(bsz, seq // tm),
        in_specs=[_tok_spec(tm, d), _mod_spec(mod4, l), _const_spec(ng), feat, feat, feat, feat,
                  _weight_spec(wg, l), _layer_spec(wa, l), _layer_spec(wb, l), _layer_spec(wc, l),
                  _weight_spec(glu, l), _weight_spec(wo, l)],
        out_specs=_tok_spec(tm, d),
        out_shape=jax.ShapeDtypeStruct((bsz, seq, d), F32),
        compiler_params=_cparams(("parallel", "parallel")),
        name="merge",
    )(x, mod4, ng, fa, fb, fc, fd, wg, wa, wb, wc, glu, wo)


def _ffn_kernel(l, final, x_ref, mod_ref, ng_ref, w1_ref, w2_ref, fg_ref, o_ref):
    mod = mod_ref[0, 0]
    x = x_ref[0]
    hb = _rms_mod(x, ng_ref[l:l + 1, :], mod[4:5, :], mod[3:4, :]).astype(BF16)
    acc = jnp.zeros(x.shape, F32)
    step = D_MODEL
    for j in range(D_FF // step):
        a = jnp.dot(hb, w1_ref[0, :, j * step:(j + 1) * step], preferred_element_type=F32)
        a = jnp.square(jnp.maximum(a, 0.0))
        acc += jnp.dot(a.astype(BF16), w2_ref[0, j * step:(j + 1) * step, :], preferred_element_type=F32)
    y = x + mod[5:6, :] * acc
    if final:
        ms = jnp.mean(y * y, axis=-1, keepdims=True)
        y = y * lax.rsqrt(ms + EPS) * fg_ref[...]
    o_ref[0] = y


def _ffn_call(l, x, mod4, ng, w1, w2, fg, final, tm):
    bsz, seq, d = x.shape
    return pl.pallas_call(
        functools.partial(_ffn_kernel, l, final),
        grid=(bsz, seq // tm),
        in_specs=[_tok_spec(tm, d), _mod_spec(mod4, l), _const_spec(ng), _weight_spec(w1, l),
                  _weight_spec(w2, l), _const_spec(fg)],
        out_specs=_tok_spec(tm, d),
        out_shape=jax.ShapeDtypeStruct((bsz, seq, d), F32),
        compiler_params=_cparams(("parallel", "parallel")),
        name="ffn",
    )(x, mod4, ng, w1, w2, fg)


def _tile(seq, want):
    return want if seq % want == 0 else seq


def kernel(x, c, ada_w, ada_b, norm_mix_g, w_in, rwkv_mu, rwkv_w0, rwkv_w2, rwkv_a0, rwkv_a2, rwkv_g2,
           rwkv_v0, rwkv_v1, rwkv_v2, rwkv_kk, rwkv_ka, rwkv_rk, rwkv_lnx_w, rwkv_lnx_b, rwkv_out,
           sg_ln_w, sg_ln_b, sg_ws, sg_bs, sg_out, conv_w, conv_out, s5_a_re, s5_a_im, s5_b_re, s5_b_im,
           s5_c_re, s5_c_im, s5_d, s5_log_dt, s5_glu_w, w_o, norm_ffn_g, ffn_w1, ffn_w2, final_g):
    in_dtype = x.dtype
    bsz, seq, d = x.shape
    depth = ada_w.shape[0]
    x = x.astype(F32)

    tt = _tile(seq, 512)
    tm = _tile(seq, 512)
    tm_ffn = _tile(seq, 1024)

    c_rows = 16
    c_pad = jnp.pad(c.astype(F32), ((0, c_rows - bsz), (0, 0)))
    mod4 = _ada_call(c_pad, ada_w.astype(F32), ada_b.astype(F32)).reshape(depth, c_rows, 6, d)

    rows = jnp.stack([rwkv_w0, rwkv_a0, rwkv_kk, rwkv_ka, rwkv_rk.reshape(depth, D_MIX), rwkv_lnx_w,
                      rwkv_lnx_b, sg_ln_w, sg_ln_b, s5_d], axis=0).astype(F32)
    sgb = jnp.repeat(jnp.swapaxes(sg_bs, 1, 2), GROUP_B, axis=2).astype(F32)
    win = w_in.astype(BF16)
    wa, wb, wc = rwkv_out.astype(BF16), sg_out.astype(BF16), conv_out.astype(BF16)
    glu = s5_glu_w.astype(BF16)
    wo = w_o.astype(BF16)
    w1 = ffn_w1.astype(BF16)
    w2 = ffn_w2.astype(BF16)
    fg = final_g.reshape(1, d).astype(F32)

    head_id = np.arange(D_MIX) // HEAD_A
    seg = jnp.asarray(head_id[:, None] == head_id[None, :], BF16)
    steps = S5_TILE // S5_SUBSEQ
    dst = np.arange(S5_TILE)
    src = (dst % S5_SUBSEQ) * steps + dst // S5_SUBSEQ
    perm = jnp.asarray(src[:, None] == np.arange(S5_TILE)[None, :], BF16)

    abre, abim, bbre, bbim = _s5_param_call(s5_a_re, s5_a_im, s5_log_dt, s5_b_re, s5_b_im)
    ab = jnp.concatenate([abre.reshape(depth, 1, S5_STATES), abim.reshape(depth, 1, S5_STATES)], axis=1)
    eye_g = jnp.asarray(np.eye(G_D), F32)

    def in_blk(bb):
        t = bb.reshape(depth, G_D, N_STATE, GROUP_D)
        return jnp.einsum('lgnc,gh->lgchn', t, eye_g).reshape(depth, D_MIX, S5_STATES)

    def out_blk(cc):
        return jnp.einsum('lgcn,gh->lgnhc', cc.astype(F32), eye_g).reshape(depth, S5_STATES, D_MIX).astype(BF16)

    bblk = jnp.concatenate([in_blk(bbre), in_blk(bbim)], axis=2).astype(BF16)
    cre = out_blk(s5_c_re)
    cim = out_blk(s5_c_im)

    f32 = lambda a: a.astype(F32)
    v_first = None
    for l in range(depth):
        outs = _mix_call(l, x, mod4, f32(norm_mix_g), win, f32(rwkv_mu), rows, f32(rwkv_w2), f32(rwkv_a2),
                         f32(rwkv_g2), seg, f32(sg_ws), sgb, f32(conv_w), perm, bblk, cre, cim, ab,
                         f32(rwkv_v0), f32(rwkv_v1), f32(rwkv_v2), v_first, tt)
        fa, fb, fc, fd = outs[:4]
        if l == 0:
            v_first = outs[4]
        x = _merge_call(l, x, mod4, f32(norm_mix_g), fa, fb, fc, fd, win, wa, wb, wc, glu, wo, tm)
        x = _ffn_call(l, x, mod4, f32(norm_ffn_g), w1, w2, fg, l == depth - 1, tm_ffn)
    return x.astype(in_dtype)
```

```python
import functools
import math

import numpy as np
import jax
import jax.numpy as jnp
from jax import lax
from jax.experimental import pallas as pl
from jax.experimental.pallas import tpu as pltpu

F32 = jnp.float32
BF16 = jnp.bfloat16

D_MODEL = 1024
N_BRANCH = 4
D_MIX = D_MODEL // N_BRANCH
HEAD_A = 64
H_A = D_MIX // HEAD_A
LORA_W = 32
LORA_A = 32
LORA_G = 64
LNX_EPS = 64e-5
CHUNK = 128
GROUP_B = 64
G_B = D_MIX // GROUP_B
CONV_K = 3
GROUP_D = 16
G_D = D_MIX // GROUP_D
N_STATE = 64
D_FF = 4 * D_MODEL
EPS = 1e-6
LN_EPS = 1e-5

A_COLS = 3 * D_MIX + LORA_W + LORA_A + LORA_G
B_COLS = 2 * D_MIX
C_COLS = 3 * D_MIX
D_COLS = D_MIX
OFF_B = A_COLS
OFF_C = OFF_B + B_COLS
OFF_D = OFF_C + C_COLS
OFF_G = OFF_D + D_COLS
LORA_COLS = LORA_W + LORA_A + LORA_G

WKV_CHUNK = 64
S5_STATES = G_D * N_STATE
S5_LANES = 2 * S5_STATES
S5_SUBSEQ = 8
S5_TILE = 256
RW_COLS = 8 * D_MIX

(ROW_W0, ROW_A0, ROW_KK, ROW_KA, ROW_RK, ROW_LNXW, ROW_LNXB, ROW_SGW, ROW_SGB, ROW_S5D) = range(10)

VMEM_LIMIT = 56 * 1024 * 1024


def _cparams(sem):
    return pltpu.CompilerParams(dimension_semantics=sem, vmem_limit_bytes=VMEM_LIMIT)


def _dot(a, b):
    return jnp.dot(a.astype(BF16), b.astype(BF16), preferred_element_type=F32)


def _dot_nt(a, b):
    return lax.dot_general(a.astype(BF16), b.astype(BF16), (((1,), (1,)), ((), ())),
                           preferred_element_type=F32)


def _dot_tn(a, b):
    return lax.dot_general(a.astype(BF16), b.astype(BF16), (((0,), (0,)), ((), ())),
                           preferred_element_type=F32)


def _split3(x):
    hi = x.astype(BF16)
    r1 = x - hi.astype(F32)
    mid = r1.astype(BF16)
    lo = (r1 - mid.astype(F32)).astype(BF16)
    return hi, mid, lo


def _split2(x):
    hi = x.astype(BF16)
    return hi, (x - hi.astype(F32)).astype(BF16)


def _dot_sel(sel, x):
    hi, mid = _split2(x)
    return jnp.dot(sel, hi, preferred_element_type=F32) + jnp.dot(sel, mid, preferred_element_type=F32)


def _dot_x_sel(x, sel):
    hi, mid = _split2(x)
    return jnp.dot(hi, sel, preferred_element_type=F32) + jnp.dot(mid, sel, preferred_element_type=F32)


def _dot3(a, b):
    ah, am, _ = _split3(a)
    bh, bm, _ = _split3(b)
    return (jnp.dot(ah, bh, preferred_element_type=F32) + jnp.dot(ah, bm, preferred_element_type=F32)
            + jnp.dot(am, bh, preferred_element_type=F32))


def _sigmoid(x):
    return 0.5 * jnp.tanh(0.5 * x) + 0.5


def _gelu_tanh(x):
    return 0.5 * x * (1.0 + jnp.tanh(math.sqrt(2.0 / math.pi) * (x + 0.044715 * (x * x * x))))


def _rms_mod(x, g, sc, sh):
    ms = jnp.mean(x * x, axis=-1, keepdims=True)
    return x * lax.rsqrt(ms + EPS) * g * (1.0 + sc) + sh


def _layer_spec(a, l):
    return pl.BlockSpec((1,) + a.shape[1:], lambda b, t: (l,) + (0,) * (a.ndim - 1))


def _weight_spec(a, l):
    return pl.BlockSpec((1,) + a.shape[1:], lambda b, t: (l,) + (0,) * (a.ndim - 1),
                        pipeline_mode=pl.Buffered(1))


def _mod_spec(mod4, l):
    return pl.BlockSpec((1, 1) + mod4.shape[2:], lambda b, t: (l, b, 0, 0))


def _const_spec(a):
    return pl.BlockSpec(a.shape, lambda b, t: (0,) * a.ndim)


def _tok_spec(tt, cols):
    return pl.BlockSpec((1, tt, cols), lambda b, t: (b, t, 0))


def _ada_kernel(c_ref, w_ref, b_ref, o_ref):
    c = c_ref[...]
    ca = c * _sigmoid(c)
    o_ref[0] = _dot3(ca, w_ref[0]) + b_ref[0]


def _ada_call(c_pad, ada_w, ada_b):
    depth, d, n = ada_w.shape
    rows = c_pad.shape[0]
    bn = 1536
    return pl.pallas_call(
        _ada_kernel,
        grid=(depth, n // bn),
        in_specs=[
            pl.BlockSpec((rows, d), lambda l, j: (0, 0)),
            pl.BlockSpec((1, d, bn), lambda l, j: (l, 0, j)),
            pl.BlockSpec((1, 1, bn), lambda l, j: (l, 0, j)),
        ],
        out_specs=pl.BlockSpec((1, rows, bn), lambda l, j: (l, 0, j)),
        out_shape=jax.ShapeDtypeStruct((depth, rows, n), F32),
        compiler_params=_cparams(("parallel", "parallel")),
        name="ada_mod",
    )(c_pad, ada_w, ada_b.reshape(depth, 1, n))


def _wkv_part(tt, rw_s, lnx_w, lnx_b, o_ref, s_ref):
    n = WKV_CHUNK
    nc = tt // n

    row = lax.broadcasted_iota(jnp.int32, (n, n), 0)
    col = lax.broadcasted_iota(jnp.int32, (n, n), 1)
    ltri = jnp.where(row >= col, 1.0, 0.0).astype(BF16)
    strict = row > col
    incl = row >= col
    eye = jnp.where(row == col, 1.0, 0.0)
    same = lambda s: jnp.right_shift(row, s) == jnp.right_shift(col, s)

    items = [(ci, hd) for ci in range(nc) for hd in range(H_A)]
    head = lambda x, hd: x[:, hd * HEAD_A:(hd + 1) * HEAD_A]

    a_t, r_t, b_t, k_t, b_h, k_h, vv, g_end = [], [], [], [], [], [], [], []
    for ci in range(nc):
        rows = slice(ci * n, (ci + 1) * n)
        r = rw_s[rows, 0 * D_MIX:1 * D_MIX]
        lw = rw_s[rows, 1 * D_MIX:2 * D_MIX]
        k = rw_s[rows, 2 * D_MIX:3 * D_MIX]
        kk = rw_s[rows, 4 * D_MIX:5 * D_MIX]
        b = rw_s[rows, 5 * D_MIX:6 * D_MIX]
        e = _dot_sel(ltri, lw)
        eg = jnp.exp(e)
        ig = jnp.exp(-e)
        ge = eg[n - 1:n, :]
        a_t.append((-kk * jnp.exp(e - lw)).astype(BF16))
        r_t.append(r * eg)
        b_t.append((b * ig).astype(BF16))
        k_t.append((k * ig).astype(BF16))
        b_h.append((b * ig * ge).astype(BF16))
        k_h.append((k * ig * ge).astype(BF16))
        vv.append(rw_s[rows, 3 * D_MIX:4 * D_MIX].astype(BF16))
        g_end.append(ge)

    gm = [_dot_nt(jnp.concatenate([head(a_t[ci], hd), head(r_t[ci], hd).astype(BF16)], axis=0),
                  jnp.concatenate([head(b_t[ci], hd), head(k_t[ci], hd)], axis=0)) for ci, hd in items]
    a_ab = [jnp.where(strict, m[:n, :n], 0.0) for m in gm]
    a_ak = [jnp.where(strict, m[:n, n:], 0.0) for m in gm]
    a_rb = [jnp.where(incl, m[n:, :n], 0.0) for m in gm]
    a_rk = [jnp.where(incl, m[n:, n:], 0.0) for m in gm]
    inv = [eye + jnp.where(same(1), a, 0.0) for a in a_ab]
    for s in range(2, int(math.log2(n)) + 1):
        off_mask = same(s) & jnp.logical_not(same(s - 1))
        tmp = [_dot(t, jnp.where(off_mask, a, 0.0)) for t, a in zip(inv, a_ab)]
        inv = [t + _dot(x, t) for t, x in zip(inv, tmp)]
    akv = [_dot(a_ak[i], head(vv[ci], hd)) for i, (ci, hd) in enumerate(items)]
    pq = [_dot(inv[i], jnp.concatenate([head(a_t[ci], hd), akv[i].astype(BF16)], axis=1))
          for i, (ci, hd) in enumerate(items)]
    mn = [_dot_tn(pq[i], head(b_h[ci], hd)) for i, (ci, hd) in enumerate(items)]
    kv = [_dot_tn(head(vv[ci], hd), head(k_h[ci], hd)) for ci, hd in items]

    state = [s_ref[hd] for hd in range(H_A)]
    starts, rq, ark = [], [], []
    for ci in range(nc):
        starts.append(list(state))
        for hd in range(H_A):
            i = ci * H_A + hd
            s0 = state[hd]
            state[hd] = s0 * head(g_end[ci], hd) + _dot(s0, mn[i][:n, :]) + (mn[i][n:, :] + kv[i])
        for hd in range(H_A):
            i = ci * H_A + hd
            rq.append(_dot(a_rb[i], pq[i]))
            ark.append(_dot(a_rk[i], head(vv[ci], hd)))
    for hd in range(H_A):
        s_ref[hd] = state[hd]

    for ci in range(nc):
        outs = []
        for hd in range(H_A):
            i = ci * H_A + hd
            s0 = starts[ci][hd]
            ro = head(r_t[ci], hd) + rq[i][:, :n]
            o = _dot_nt(ro, s0) + rq[i][:, n:] + ark[i]
            mu = jnp.mean(o, axis=-1, keepdims=True)
            var = jnp.mean(jnp.square(o - mu), axis=-1, keepdims=True)
            outs.append((o - mu) * lax.rsqrt(var + LNX_EPS))
        rows = slice(ci * n, (ci + 1) * n)
        on = jnp.concatenate(outs, axis=1) * lnx_w + lnx_b
        g = rw_s[rows, 6 * D_MIX:7 * D_MIX]
        bonus = rw_s[rows, 7 * D_MIX:8 * D_MIX]
        o_ref[0, rows, :] = ((on + bonus) * g).astype(BF16)


def _s5_part(u, perm, bblk, cre_ref, cim_ref, ab_re, ab_im, d_row, bu_s, x_s, carry_s):
    half = S5_STATES
    steps = S5_TILE // S5_SUBSEQ

    u_p = _dot_sel(perm, u)
    bu_s[...] = jnp.dot(u_p.astype(BF16), bblk, preferred_element_type=F32)

    a_re = jnp.broadcast_to(ab_re, (S5_SUBSEQ, half))
    a_im = jnp.broadcast_to(ab_im, (S5_SUBSEQ, half))

    def step(i, st):
        s_re, s_im = st
        rows = slice(i * S5_SUBSEQ, (i + 1) * S5_SUBSEQ)
        n_re = a_re * s_re - a_im * s_im + bu_s[rows, 0:half]
        n_im = a_re * s_im + a_im * s_re + bu_s[rows, half:]
        return n_re, n_im

    zero = jnp.zeros((S5_SUBSEQ, half), F32)
    st = (zero, zero)
    for i in range(steps):
        st = step(i, st)
    e_re, e_im = st

    p_re, p_im = ab_re, ab_im
    for _ in range(int(math.log2(steps))):
        p_re, p_im = p_re * p_re - p_im * p_im, 2.0 * p_re * p_im

    c_re, c_im = carry_s[0:1, :], carry_s[1:2, :]
    in_re, in_im = [], []
    for j in range(S5_SUBSEQ):
        in_re.append(c_re)
        in_im.append(c_im)
        c_re, c_im = (p_re * c_re - p_im * c_im + e_re[j:j + 1, :],
                      p_re * c_im + p_im * c_re + e_im[j:j + 1, :])
    carry_s[0:1, :] = c_re
    carry_s[1:2, :] = c_im

    st = (jnp.concatenate(in_re, axis=0), jnp.concatenate(in_im, axis=0))
    for i in range(steps):
        st = step(i, st)
        rows = slice(i * S5_SUBSEQ, (i + 1) * S5_SUBSEQ)
        x_s[rows, 0:half] = st[0]
        x_s[rows, half:] = st[1]

    y = (jnp.dot(x_s[:, 0:half].astype(BF16), cre_ref[0], preferred_element_type=F32)
         - jnp.dot(x_s[:, half:].astype(BF16), cim_ref[0], preferred_element_type=F32))
    f_p = _gelu_tanh(y + d_row * u_p).astype(BF16)
    return lax.dot_general(perm, f_p, (((0,), (0,)), ((), ())), preferred_element_type=F32).astype(BF16)


def _mix_kernel(l, tt, *refs):
    has_vmix = l > 0
    refs = list(refs)
    (x_ref, mod_ref, ng_ref, win_ref, mu_ref, rows_ref, w2_ref, a2_ref, g2_ref, seg_ref, ws_ref, sgb_ref,
     cw_ref, perm_ref, bblk_ref, cre_ref, cim_ref, ab_ref) = refs[:18]
    refs = refs[18:]
    if has_vmix:
        vf_ref, v0_ref, v1_ref, v2_ref = refs[:4]
        refs = refs[4:]
        fa_ref, fb_ref, fc_ref, fd_ref = refs[:4]
        refs = refs[4:]
    else:
        fa_ref, fb_ref, fc_ref, fd_ref, vout_ref = refs[:5]
        refs = refs[5:]
    pa_s, z_s, rw_s, s_ref, bu_s, x_s, carry_s = refs
    prow = lambda i: rows_ref[i, l:l + 1, :]

    @pl.when(pl.program_id(1) == 0)
    def _():
        pa_s[0:8, :] = jnp.zeros((8, A_COLS), F32)
        z_s[0:8, :] = jnp.zeros((8, D_MIX), F32)
        s_ref[...] = jnp.zeros(s_ref.shape, F32)
        carry_s[...] = jnp.zeros(carry_s.shape, F32)

    mod = mod_ref[0, 0]
    h = _rms_mod(x_ref[0], ng_ref[l:l + 1, :], mod[1:2, :], mod[0:1, :])
    p = jnp.dot(h.astype(BF16), win_ref[0], preferred_element_type=F32)

    pa_s[8:8 + tt, :] = p[:, :A_COLS]
    pa = p[:, :A_COLS]
    prev = pa_s[7:7 + tt, :]
    pa = pa + (prev - pa) * mu_ref[l:l + 1, :]
    pa_s[0:8, :] = pa_s[tt:tt + 8, :]
    r = pa[:, 0:D_MIX]
    k = pa[:, D_MIX:2 * D_MIX]
    v = pa[:, 2 * D_MIX:3 * D_MIX]
    lora = pa[:, 3 * D_MIX:A_COLS]

    def lora_w(w_ref, start):
        w = w_ref[0]
        parts = []
        if start:
            parts.append(jnp.zeros((start, D_MIX), F32))
        parts.append(w)
        if LORA_COLS - start - w.shape[0]:
            parts.append(jnp.zeros((LORA_COLS - start - w.shape[0], D_MIX), F32))
        return jnp.concatenate(parts, axis=0).astype(BF16)

    lw = -math.exp(-0.5) * _sigmoid(prow(ROW_W0) + _dot(jnp.tanh(lora), lora_w(w2_ref, 0)))
    if has_vmix:
        vgate = _sigmoid(v0_ref[l - 1:l, :] + _dot(_dot(v, v1_ref[0]), v2_ref[0]))
        v = v + (vf_ref[0] - v) * vgate
    else:
        vout_ref[0] = v
    a = _sigmoid(prow(ROW_A0) + _dot(lora, lora_w(a2_ref, LORA_W)))
    g = _dot(_sigmoid(lora), lora_w(g2_ref, LORA_W + LORA_A))
    kk = k * prow(ROW_KK)
    seg = seg_ref[...]
    kk_norm = jnp.sqrt(_dot_x_sel(kk * kk, seg))
    kk = kk / jnp.maximum(kk_norm, 1e-12)
    k = k * (1.0 + (a - 1.0) * prow(ROW_KA))
    bonus = _dot_x_sel(r * k * prow(ROW_RK), seg) * v
    rw_s[:, 0 * D_MIX:1 * D_MIX] = r
    rw_s[:, 1 * D_MIX:2 * D_MIX] = lw
    rw_s[:, 2 * D_MIX:3 * D_MIX] = k
    rw_s[:, 3 * D_MIX:4 * D_MIX] = v
    rw_s[:, 4 * D_MIX:5 * D_MIX] = kk
    rw_s[:, 5 * D_MIX:6 * D_MIX] = kk * a
    rw_s[:, 6 * D_MIX:7 * D_MIX] = g
    rw_s[:, 7 * D_MIX:8 * D_MIX] = bonus

    z = _gelu_tanh(p[:, OFF_B:OFF_C])
    su = z[:, :D_MIX]
    sv = z[:, D_MIX:]
    mu_v = jnp.mean(sv, axis=-1, keepdims=True)
    var_v = jnp.mean(jnp.square(sv - mu_v), axis=-1, keepdims=True)
    sv = (sv - mu_v) * lax.rsqrt(var_v + LN_EPS) * prow(ROW_SGW) + prow(ROW_SGB)
    row = lax.broadcasted_iota(jnp.int32, (CHUNK, CHUNK), 0)
    col = lax.broadcasted_iota(jnp.int32, (CHUNK, CHUNK), 1)
    causal = row >= col
    wsm = [jnp.where(causal, ws_ref[0, gi], 0.0).astype(BF16) for gi in range(G_B)]
    sv_b = sv.astype(BF16)
    for n in range(tt // CHUNK):
        rows = slice(n * CHUNK, (n + 1) * CHUNK)
        mixed = jnp.concatenate(
            [jnp.dot(wsm[gi], sv_b[rows, gi * GROUP_B:(gi + 1) * GROUP_B], preferred_element_type=F32)
             for gi in range(G_B)], axis=1) + sgb_ref[0]
        fb_ref[0, rows, :] = (su[rows, :] * mixed).astype(BF16)

    pc = p[:, OFF_C:OFF_D]
    bg = pc[:, :D_MIX]
    zc = pc[:, D_MIX:2 * D_MIX] * pc[:, 2 * D_MIX:]
    z_s[8:8 + tt, :] = zc
    y = (cw_ref[l, 0:1, :] * z_s[6:6 + tt, :] + cw_ref[l, 1:2, :] * z_s[7:7 + tt, :]
         + cw_ref[l, 2:3, :] * zc)
    z_s[0:8, :] = z_s[tt:tt + 8, :]
    fc_ref[0] = (bg * y).astype(BF16)

    u5 = p[:, OFF_D:OFF_G]

    _wkv_part(tt, rw_s, prow(ROW_LNXW), prow(ROW_LNXB), fa_ref, s_ref)

    for n in range(tt // S5_TILE):
        rows = slice(n * S5_TILE, (n + 1) * S5_TILE)
        fd_ref[0, rows, :] = _s5_part(u5[rows, :], perm_ref[...], bblk_ref[0], cre_ref, cim_ref,
                                      ab_ref[0, 0:1, :], ab_ref[0, 1:2, :], prow(ROW_S5D),
                                      bu_s, x_s, carry_s)


def _mix_call(l, x, mod4, ng, win, mu, rows, w2, a2, g2, seg, ws, sgb, cw, perm, bblk, cre, cim, ab,
              v0, v1, v2, v_first, tt):
    bsz, seq, d = x.shape
    has_vmix = l > 0
    ins = [x, mod4, ng, win, mu, rows, w2, a2, g2, seg, ws, sgb, cw, perm, bblk, cre, cim, ab]
    in_specs = [
        _tok_spec(tt, d), _mod_spec(mod4, l), _const_spec(ng),
        pl.BlockSpec((1, d, OFF_G), lambda b, t: (l, 0, 0)),
        _const_spec(mu), _const_spec(rows), _layer_spec(w2, l), _layer_spec(a2, l), _layer_spec(g2, l),
        _const_spec(seg), _layer_spec(ws, l), _layer_spec(sgb, l), _const_spec(cw), _const_spec(perm),
        _layer_spec(bblk, l), _layer_spec(cre, l), _layer_spec(cim, l), _layer_spec(ab, l),
    ]
    feat = jax.ShapeDtypeStruct((bsz, seq, D_MIX), BF16)
    out_shape = [feat, feat, feat, feat]
    out_specs = [_tok_spec(tt, D_MIX)] * 4
    if has_vmix:
        ins += [v_first, v0, v1, v2]
        in_specs += [_tok_spec(tt, D_MIX), _const_spec(v0), _layer_spec(v1, l - 1), _layer_spec(v2, l - 1)]
    else:
        out_shape.append(jax.ShapeDtypeStruct((bsz, seq, D_MIX), F32))
        out_specs.append(_tok_spec(tt, D_MIX))
    return pl.pallas_call(
        functools.partial(_mix_kernel, l, tt),
        grid=(bsz, seq // tt),
        in_specs=in_specs,
        out_specs=out_specs,
        out_shape=out_shape,
        scratch_shapes=[pltpu.VMEM((tt + 8, A_COLS), F32), pltpu.VMEM((tt + 8, D_MIX), F32),
                        pltpu.VMEM((tt, RW_COLS), F32), pltpu.VMEM((H_A, HEAD_A, HEAD_A), F32),
                        pltpu.VMEM((S5_TILE, S5_LANES), F32), pltpu.VMEM((S5_TILE, S5_LANES), F32),
                        pltpu.VMEM((8, S5_STATES), F32)],
        compiler_params=_cparams(("parallel", "arbitrary")),
        name="mix",
    )(*ins)


def _s5_param_kernel(are_ref, aim_ref, ldt_ref, bre_ref, bim_ref, abre_ref, abim_ref, bbre_ref, bbim_ref):
    lam_re = jnp.minimum(are_ref[...], -1e-4)
    lam_im = aim_ref[...]
    dt = jnp.exp(ldt_ref[...])
    mag = jnp.exp(lam_re * dt)
    ab_re = mag * jnp.cos(lam_im * dt)
    ab_im = mag * jnp.sin(lam_im * dt)
    den = lam_re * lam_re + lam_im * lam_im
    q_re = ((ab_re - 1.0) * lam_re + ab_im * lam_im) / den
    q_im = (ab_im * lam_re - (ab_re - 1.0) * lam_im) / den
    abre_ref[...] = ab_re
    abim_ref[...] = ab_im
    b_re = bre_ref[...]
    b_im = bim_ref[...]
    bbre_ref[...] = q_re * b_re - q_im * b_im
    bbim_ref[...] = q_re * b_im + q_im * b_re


def _s5_param_call(a_re, a_im, log_dt, b_re, b_im):
    rows = a_re.size
    col = lambda a: a.reshape(rows, 1).astype(F32)
    ldt = jnp.repeat(log_dt.astype(F32), N_STATE, axis=-1).reshape(rows, 1)
    return pl.pallas_call(
        _s5_param_kernel,
        out_shape=(jax.ShapeDtypeStruct((rows, 1), F32), jax.ShapeDtypeStruct((rows, 1), F32),
                   jax.ShapeDtypeStruct((rows, GROUP_D), F32), jax.ShapeDtypeStruct((rows, GROUP_D), F32)),
        name="s5_params",
    )(col(a_re), col(a_im), ldt, b_re.reshape(rows, GROUP_D).astype(F32),
      b_im.reshape(rows, GROUP_D).astype(F32))


def _merge_kernel(l, x_ref, mod_ref, ng_ref, fa_ref, fb_ref, fc_ref, fd_ref, wg_ref, wa_ref, wb_ref, wc_ref,
                  glu_ref, wo_ref, o_ref):
    d = D_MODEL
    mod = mod_ref[0, 0]
    x = x_ref[0]
    hb = _rms_mod(x, ng_ref[l:l + 1, :], mod[1:2, :], mod[0:1, :]).astype(BF16)

    def gate(i):
        cols = slice(OFF_G + i * d, OFF_G + (i + 1) * d)
        return _sigmoid(jnp.dot(hb, wg_ref[0, :, cols], preferred_element_type=F32))

    merged = gate(0) * jnp.dot(fa_ref[0], wa_ref[0], preferred_element_type=F32)
    merged += gate(1) * jnp.dot(fb_ref[0], wb_ref[0], preferred_element_type=F32)
    merged += gate(2) * jnp.dot(fc_ref[0], wc_ref[0], preferred_element_type=F32)
    hd = jnp.dot(fd_ref[0], glu_ref[0], preferred_element_type=F32)
    merged += gate(3) * (hd[:, :d] * _sigmoid(hd[:, d:]))
    o_ref[0] = x + mod[2:3, :] * jnp.dot(merged.astype(BF16), wo_ref[0], preferred_element_type=F32)


def _merge_call(l, x, mod4, ng, fa, fb, fc, fd, wg, wa, wb, wc, glu, wo, tm):
    bsz, seq, d = x.shape
    feat = _tok_spec(tm, D_MIX)
    return pl.pallas_call(
        functools.partial(_merge_kernel, l),
        grid=(bsz, seq // tm),
        in_specs=[_tok_spec(tm, d), _mod_spec(mod4, l), _const_spec(ng), feat, feat, feat, feat,
                  _weight_spec(wg, l), _layer_spec(wa, l), _layer_spec(wb, l), _layer_spec(wc, l),
                  _weight_spec(glu, l), _weight_spec(wo, l)],
        out_specs=_tok_spec(tm, d),
        out_shape=jax.ShapeDtypeStruct((bsz, seq, d), F32),
        compiler_params=_cparams(("parallel", "parallel")),
        name="merge",
    )(x, mod4, ng, fa, fb, fc, fd, wg, wa, wb, wc, glu, wo)


def _ffn_kernel(l, final, x_ref, mod_ref, ng_ref, w1_ref, w2_ref, fg_ref, o_ref):
    mod = mod_ref[0, 0]
    x = x_ref[0]
    hb = _rms_mod(x, ng_ref[l:l + 1, :], mod[4:5, :], mod[3:4, :]).astype(BF16)
    acc = jnp.zeros(x.shape, F32)
    step = D_MODEL
    for j in range(D_FF // step):
        a = jnp.dot(hb, w1_ref[0, :, j * step:(j + 1) * step], preferred_element_type=F32)
        a = jnp.square(jnp.maximum(a, 0.0))
        acc += jnp.dot(a.astype(BF16), w2_ref[0, j * step:(j + 1) * step, :], preferred_element_type=F32)
    y = x + mod[5:6, :] * acc
    if final:
        ms = jnp.mean(y * y, axis=-1, keepdims=True)
        y = y * lax.rsqrt(ms + EPS) * fg_ref[...]
    o_ref[0] = y


def _ffn_call(l, x, mod4, ng, w1, w2, fg, final, tm):
    bsz, seq, d = x.shape
    return pl.pallas_call(
        functools.partial(_ffn_kernel, l, final),
        grid=(bsz, seq // tm),
        in_specs=[_tok_spec(tm, d), _mod_spec(mod4, l), _const_spec(ng), _weight_spec(w1, l),
                  _weight_spec(w2, l), _const_spec(fg)],
        out_specs=_tok_spec(tm, d),
        out_shape=jax.ShapeDtypeStruct((bsz, seq, d), F32),
        compiler_params=_cparams(("parallel", "parallel")),
        name="ffn",
    )(x, mod4, ng, w1, w2, fg)


def _tile(seq, want):
    return want if seq % want == 0 else seq


def kernel(x, c, ada_w, ada_b, norm_mix_g, w_in, rwkv_mu, rwkv_w0, rwkv_w2, rwkv_a0, rwkv_a2, rwkv_g2,
           rwkv_v0, rwkv_v1, rwkv_v2, rwkv_kk, rwkv_ka, rwkv_rk, rwkv_lnx_w, rwkv_lnx_b, rwkv_out,
           sg_ln_w, sg_ln_b, sg_ws, sg_bs, sg_out, conv_w, conv_out, s5_a_re, s5_a_im, s5_b_re, s5_b_im,
           s5_c_re, s5_c_im, s5_d, s5_log_dt, s5_glu_w, w_o, norm_ffn_g, ffn_w1, ffn_w2, final_g):
    in_dtype = x.dtype
    bsz, seq, d = x.shape
    depth = ada_w.shape[0]
    x = x.astype(F32)

    tt = _tile(seq, 512)
    tm = _tile(seq, 512)
    tm_ffn = _tile(seq, 1024)

    c_rows = 16
    c_pad = jnp.pad(c.astype(F32), ((0, c_rows - bsz), (0, 0)))
    mod4 = _ada_call(c_pad, ada_w.astype(F32), ada_b.astype(F32)).reshape(depth, c_rows, 6, d)

    rows = jnp.stack([rwkv_w0, rwkv_a0, rwkv_kk, rwkv_ka, rwkv_rk.reshape(depth, D_MIX), rwkv_lnx_w,
                      rwkv_lnx_b, sg_ln_w, sg_ln_b, s5_d], axis=0).astype(F32)
    sgb = jnp.repeat(jnp.swapaxes(sg_bs, 1, 2), GROUP_B, axis=2).astype(F32)
    win = w_in.astype(BF16)
    wa, wb, wc = rwkv_out.astype(BF16), sg_out.astype(BF16), conv_out.astype(BF16)
    glu = s5_glu_w.astype(BF16)
    wo = w_o.astype(BF16)
    w1 = ffn_w1.astype(BF16)
    w2 = ffn_w2.astype(BF16)
    fg = final_g.reshape(1, d).astype(F32)

    head_id = np.arange(D_MIX) // HEAD_A
    seg = jnp.asarray(head_id[:, None] == head_id[None, :], BF16)
    steps = S5_TILE // S5_SUBSEQ
    dst = np.arange(S5_TILE)
    src = (dst % S5_SUBSEQ) * steps + dst // S5_SUBSEQ
    perm = jnp.asarray(src[:, None] == np.arange(S5_TILE)[None, :], BF16)

    abre, abim, bbre, bbim = _s5_param_call(s5_a_re, s5_a_im, s5_log_dt, s5_b_re, s5_b_im)
    ab = jnp.concatenate([abre.reshape(depth, 1, S5_STATES), abim.reshape(depth, 1, S5_STATES)], axis=1)
    eye_g = jnp.asarray(np.eye(G_D), F32)

    def in_blk(bb):
        t = bb.reshape(depth, G_D, N_STATE, GROUP_D)
        return jnp.einsum('lgnc,gh->lgchn', t, eye_g).reshape(depth, D_MIX, S5_STATES)

    def out_blk(cc):
        return jnp.einsum('lgcn,gh->lgnhc', cc.astype(F32), eye_g).reshape(depth, S5_STATES, D_MIX).astype(BF16)

    bblk = jnp.concatenate([in_blk(bbre), in_blk(bbim)], axis=2).astype(BF16)
    cre = out_blk(s5_c_re)
    cim = out_blk(s5_c_im)

    f32 = lambda a: a.astype(F32)
    v_first = None
    for l in range(depth):
        outs = _mix_call(l, x, mod4, f32(norm_mix_g), win, f32(rwkv_mu), rows, f32(rwkv_w2), f32(rwkv_a2),
                         f32(rwkv_g2), seg, f32(sg_ws), sgb, f32(conv_w), perm, bblk, cre, cim, ab,
                         f32(rwkv_v0), f32(rwkv_v1), f32(rwkv_v2), v_first, tt)
        fa, fb, fc, fd = outs[:4]
        if l == 0:
            v_first = outs[4]
        x = _merge_call(l, x, mod4, f32(norm_mix_g), fa, fb, fc, fd, win, wa, wb, wc, glu, wo, tm)
        x = _ffn_call(l, x, mod4, f32(norm_ffn_g), w1, w2, fg, l == depth - 1, tm_ffn)
    return x.astype(in_dtype)
```

```python
import functools
import math

import numpy as np
import jax
import jax.numpy as jnp
from jax import lax
from jax.experimental import pallas as pl
from jax.experimental.pallas import tpu as pltpu

F32 = jnp.float32
BF16 = jnp.bfloat16

D_MODEL = 1024
N_BRANCH = 4
D_MIX = D_MODEL // N_BRANCH
HEAD_A = 64
H_A = D_MIX // HEAD_A
LORA_W = 32
LORA_A = 32
LORA_G = 64
LNX_EPS = 64e-5
CHUNK = 128
GROUP_B = 64
G_B = D_MIX // GROUP_B
CONV_K = 3
GROUP_D = 16
G_D = D_MIX // GROUP_D
N_STATE = 64
D_FF = 4 * D_MODEL
EPS = 1e-6
LN_EPS = 1e-5

A_COLS = 3 * D_MIX + LORA_W + LORA_A + LORA_G
B_COLS = 2 * D_MIX
C_COLS = 3 * D_MIX
D_COLS = D_MIX
OFF_B = A_COLS
OFF_C = OFF_B + B_COLS
OFF_D = OFF_C + C_COLS
OFF_G = OFF_D + D_COLS
LORA_COLS = LORA_W + LORA_A + LORA_G

WKV_CHUNK = 64
S5_STATES = G_D * N_STATE
S5_LANES = 2 * S5_STATES
S5_SUBSEQ = 8
S5_TILE = 256
RW_COLS = 8 * D_MIX

(ROW_W0, ROW_A0, ROW_KK, ROW_KA, ROW_RK, ROW_LNXW, ROW_LNXB, ROW_SGW, ROW_SGB, ROW_S5D) = range(10)

VMEM_LIMIT = 56 * 1024 * 1024


def _cparams(sem):
    return pltpu.CompilerParams(dimension_semantics=sem, vmem_limit_bytes=VMEM_LIMIT)


def _dot(a, b):
    return jnp.dot(a.astype(BF16), b.astype(BF16), preferred_element_type=F32)


def _dot_nt(a, b):
    return lax.dot_general(a.astype(BF16), b.astype(BF16), (((1,), (1,)), ((), ())),
                           preferred_element_type=F32)


def _dot_tn(a, b):
    return lax.dot_general(a.astype(BF16), b.astype(BF16), (((0,), (0,)), ((), ())),
                           preferred_element_type=F32)


def _split3(x):
    hi = x.astype(BF16)
    r1 = x - hi.astype(F32)
    mid = r1.astype(BF16)
    lo = (r1 - mid.astype(F32)).astype(BF16)
    return hi, mid, lo


def _split2(x):
    hi = x.astype(BF16)
    return hi, (x - hi.astype(F32)).astype(BF16)


def _dot_sel(sel, x):
    hi, mid = _split2(x)
    return jnp.dot(sel, hi, preferred_element_type=F32) + jnp.dot(sel, mid, preferred_element_type=F32)


def _dot_x_sel(x, sel):
    hi, mid = _split2(x)
    return jnp.dot(hi, sel, preferred_element_type=F32) + jnp.dot(mid, sel, preferred_element_type=F32)


def _dot3(a, b):
    ah, am, _ = _split3(a)
    bh, bm, _ = _split3(b)
    return (jnp.dot(ah, bh, preferred_element_type=F32) + jnp.dot(ah, bm, preferred_element_type=F32)
            + jnp.dot(am, bh, preferred_element_type=F32))


def _sigmoid(x):
    return 0.5 * jnp.tanh(0.5 * x) + 0.5


def _gelu_tanh(x):
    return 0.5 * x * (1.0 + jnp.tanh(math.sqrt(2.0 / math.pi) * (x + 0.044715 * (x * x * x))))


def _rms_mod(x, g, sc, sh):
    ms = jnp.mean(x * x, axis=-1, keepdims=True)
    return x * lax.rsqrt(ms + EPS) * g * (1.0 + sc) + sh


def _layer_spec(a, l):
    return pl.BlockSpec((1,) + a.shape[1:], lambda b, t: (l,) + (0,) * (a.ndim - 1))


def _weight_spec(a, l):
    return pl.BlockSpec((1,) + a.shape[1:], lambda b, t: (l,) + (0,) * (a.ndim - 1),
                        pipeline_mode=pl.Buffered(1))


def _mod_spec(mod4, l):
    return pl.BlockSpec((1, 1) + mod4.shape[2:], lambda b, t: (l, b, 0, 0))


def _const_spec(a):
    return pl.BlockSpec(a.shape, lambda b, t: (0,) * a.ndim)


def _tok_spec(tt, cols):
    return pl.BlockSpec((1, tt, cols), lambda b, t: (b, t, 0))


def _ada_kernel(c_ref, w_ref, b_ref, o_ref):
    c = c_ref[...]
    ca = c * _sigmoid(c)
    o_ref[0] = _dot3(ca, w_ref[0]) + b_ref[0]


def _ada_call(c_pad, ada_w, ada_b):
    depth, d, n = ada_w.shape
    rows = c_pad.shape[0]
    bn = 1536
    return pl.pallas_call(
        _ada_kernel,
        grid=(depth, n // bn),
        in_specs=[
            pl.BlockSpec((rows, d), lambda l, j: (0, 0)),
            pl.BlockSpec((1, d, bn), lambda l, j: (l, 0, j)),
            pl.BlockSpec((1, 1, bn), lambda l, j: (l, 0, j)),
        ],
        out_specs=pl.BlockSpec((1, rows, bn), lambda l, j: (l, 0, j)),
        out_shape=jax.ShapeDtypeStruct((depth, rows, n), F32),
        compiler_params=_cparams(("parallel", "parallel")),
        name="ada_mod",
    )(c_pad, ada_w, ada_b.reshape(depth, 1, n))


def _wkv_part(tt, rw_s, lnx_w, lnx_b, o_ref, s_ref):
    n = WKV_CHUNK
    nc = tt // n

    row = lax.broadcasted_iota(jnp.int32, (n, n), 0)
    col = lax.broadcasted_iota(jnp.int32, (n, n), 1)
    ltri = jnp.where(row >= col, 1.0, 0.0).astype(BF16)
    strict = row > col
    incl = row >= col
    eye = jnp.where(row == col, 1.0, 0.0)
    same = lambda s: jnp.right_shift(row, s) == jnp.right_shift(col, s)

    items = [(ci, hd) for ci in range(nc) for hd in range(H_A)]
    head = lambda x, hd: x[:, hd * HEAD_A:(hd + 1) * HEAD_A]

    a_t, r_t, b_t, k_t, b_h, k_h, vv, g_end = [], [], [], [], [], [], [], []
    for ci in range(nc):
        rows = slice(ci * n, (ci + 1) * n)
        r = rw_s[rows, 0 * D_MIX:1 * D_MIX]
        lw = rw_s[rows, 1 * D_MIX:2 * D_MIX]
        k = rw_s[rows, 2 * D_MIX:3 * D_MIX]
        kk = rw_s[rows, 4 * D_MIX:5 * D_MIX]
        b = rw_s[rows, 5 * D_MIX:6 * D_MIX]
        e = _dot_sel(ltri, lw)
        eg = jnp.exp(e)
        ig = jnp.exp(-e)
        ge = eg[n - 1:n, :]
        a_t.append((-kk * jnp.exp(e - lw)).astype(BF16))
        r_t.append(r * eg)
        b_t.append((b * ig).astype(BF16))
        k_t.append((k * ig).astype(BF16))
        b_h.append((b * ig * ge).astype(BF16))
        k_h.append((k * ig * ge).astype(BF16))
        vv.append(rw_s[rows, 3 * D_MIX:4 * D_MIX].astype(BF16))
        g_end.append(ge)

    gm = [_dot_nt(jnp.concatenate([head(a_t[ci], hd), head(r_t[ci], hd).astype(BF16)], axis=0),
                  jnp.concatenate([head(b_t[ci], hd), head(k_t[ci], hd)], axis=0)) for ci, hd in items]
    a_ab = [jnp.where(strict, m[:n, :n], 0.0) for m in gm]
    a_ak = [jnp.where(strict, m[:n, n:], 0.0) for m in gm]
    a_rb = [jnp.where(incl, m[n:, :n], 0.0) for m in gm]
    a_rk = [jnp.where(incl, m[n:, n:], 0.0) for m in gm]
    inv = [eye + jnp.where(same(1), a, 0.0) for a in a_ab]
    for s in range(2, int(math.log2(n)) + 1):
        off_mask = same(s) & jnp.logical_not(same(s - 1))
        tmp = [_dot(t, jnp.where(off_mask, a, 0.0)) for t, a in zip(inv, a_ab)]
        inv = [t + _dot(x, t) for t, x in zip(inv, tmp)]
    akv = [_dot(a_ak[i], head(vv[ci], hd)) for i, (ci, hd) in enumerate(items)]
    pq = [_dot(inv[i], jnp.concatenate([head(a_t[ci], hd), akv[i].astype(BF16)], axis=1))
          for i, (ci, hd) in enumerate(items)]
    mn = [_dot_tn(pq[i], head(b_h[ci], hd)) for i, (ci, hd) in enumerate(items)]
    kv = [_dot_tn(head(vv[ci], hd), head(k_h[ci], hd)) for ci, hd in items]

    state = [s_ref[hd] for hd in range(H_A)]
    starts, rq, ark = [], [], []
    for ci in range(nc):
        starts.append(list(state))
        for hd in range(H_A):
            i = ci * H_A + hd
            s0 = state[hd]
            state[hd] = s0 * head(g_end[ci], hd) + _dot(s0, mn[i][:n, :]) + (mn[i][n:, :] + kv[i])
        for hd in range(H_A):
            i = ci * H_A + hd
            rq.append(_dot(a_rb[i], pq[i]))
            ark.append(_dot(a_rk[i], head(vv[ci], hd)))
    for hd in range(H_A):
        s_ref[hd] = state[hd]

    for ci in range(nc):
        outs = []
        for hd in range(H_A):
            i = ci * H_A + hd
            s0 = starts[ci][hd]
            ro = head(r_t[ci], hd) + rq[i][:, :n]
            o = _dot_nt(ro, s0) + rq[i][:, n:] + ark[i]
            mu = jnp.mean(o, axis=-1, keepdims=True)
            var = jnp.mean(jnp.square(o - mu), axis=-1, keepdims=True)
            outs.append((o - mu) * lax.rsqrt(var + LNX_EPS))
        rows = slice(ci * n, (ci + 1) * n)
        on = jnp.concatenate(outs, axis=1) * lnx_w + lnx_b
        g = rw_s[rows, 6 * D_MIX:7 * D_MIX]
        bonus = rw_s[rows, 7 * D_MIX:8 * D_MIX]
        o_ref[0, rows, :] = ((on + bonus) * g).astype(BF16)


def _s5_part(u5, perm, bblk, cre_ref, cim_ref, ab_re, ab_im, d_row, o_ref, bu_s, x_s, carry_s):
    half = S5_STATES
    steps = S5_TILE // S5_SUBSEQ
    nt = u5.shape[0] // S5_TILE

    u_p = [_dot_sel(perm, u5[n * S5_TILE:(n + 1) * S5_TILE, :]) for n in range(nt)]
    for n in range(nt):
        bu_s[n] = jnp.dot(u_p[n].astype(BF16), bblk, preferred_element_type=F32)

    a_re = jnp.broadcast_to(ab_re, (S5_SUBSEQ, half))
    a_im = jnp.broadcast_to(ab_im, (S5_SUBSEQ, half))

    def step(n, i, st):
        s_re, s_im = st
        rows = slice(i * S5_SUBSEQ, (i + 1) * S5_SUBSEQ)
        n_re = a_re * s_re - a_im * s_im + bu_s[n, rows, 0:half]
        n_im = a_re * s_im + a_im * s_re + bu_s[n, rows, half:]
        return n_re, n_im

    zero = jnp.zeros((S5_SUBSEQ, half), F32)
    st = [(zero, zero)] * nt
    for i in range(steps):
        st = [step(n, i, st[n]) for n in range(nt)]
    ends = st

    p_re, p_im = ab_re, ab_im
    for _ in range(int(math.log2(steps))):
        p_re, p_im = p_re * p_re - p_im * p_im, 2.0 * p_re * p_im

    c_re, c_im = carry_s[0:1, :], carry_s[1:2, :]
    st = []
    for n in range(nt):
        e_re, e_im = ends[n]
        in_re, in_im = [], []
        for j in range(S5_SUBSEQ):
            in_re.append(c_re)
            in_im.append(c_im)
            c_re, c_im = (p_re * c_re - p_im * c_im + e_re[j:j + 1, :],
                          p_re * c_im + p_im * c_re + e_im[j:j + 1, :])
        st.append((jnp.concatenate(in_re, axis=0), jnp.concatenate(in_im, axis=0)))
    carry_s[0:1, :] = c_re
    carry_s[1:2, :] = c_im

    for i in range(steps):
        st = [step(n, i, st[n]) for n in range(nt)]
        rows = slice(i * S5_SUBSEQ, (i + 1) * S5_SUBSEQ)
        for n in range(nt):
            x_s[n, rows, 0:half] = st[n][0]
            x_s[n, rows, half:] = st[n][1]

    for n in range(nt):
        y = (jnp.dot(x_s[n, :, 0:half].astype(BF16), cre_ref[0], preferred_element_type=F32)
             - jnp.dot(x_s[n, :, half:].astype(BF16), cim_ref[0], preferred_element_type=F32))
        f_p = _gelu_tanh(y + d_row * u_p[n]).astype(BF16)
        o_ref[0, n * S5_TILE:(n + 1) * S5_TILE, :] = lax.dot_general(
            perm, f_p, (((0,), (0,)), ((), ())), preferred_element_type=F32).astype(BF16)


def _mix_kernel(l, tt, *refs):
    has_vmix = l > 0
    refs = list(refs)
    (x_ref, mod_ref, ng_ref, win_ref, mu_ref, rows_ref, w2_ref, a2_ref, g2_ref, seg_ref, ws_ref, sgb_ref,
     cw_ref, perm_ref, bblk_ref, cre_ref, cim_ref, ab_ref) = refs[:18]
    refs = refs[18:]
    if has_vmix:
        vf_ref, v0_ref, v1_ref, v2_ref = refs[:4]
        refs = refs[4:]
        fa_ref, fb_ref, fc_ref, fd_ref = refs[:4]
        refs = refs[4:]
    else:
        fa_ref, fb_ref, fc_ref, fd_ref, vout_ref = refs[:5]
        refs = refs[5:]
    pa_s, z_s, rw_s, s_ref, bu_s, x_s, carry_s = refs
    prow = lambda i: rows_ref[i, l:l + 1, :]

    @pl.when(pl.program_id(1) == 0)
    def _():
        pa_s[0:8, :] = jnp.zeros((8, A_COLS), F32)
        z_s[0:8, :] = jnp.zeros((8, D_MIX), F32)
        s_ref[...] = jnp.zeros(s_ref.shape, F32)
        carry_s[...] = jnp.zeros(carry_s.shape, F32)

    mod = mod_ref[0, 0]
    h = _rms_mod(x_ref[0], ng_ref[l:l + 1, :], mod[1:2, :], mod[0:1, :])
    p = jnp.dot(h.astype(BF16), win_ref[0], preferred_element_type=F32)

    pa_s[8:8 + tt, :] = p[:, :A_COLS]
    pa = p[:, :A_COLS]
    prev = pa_s[7:7 + tt, :]
    pa = pa + (prev - pa) * mu_ref[l:l + 1, :]
    pa_s[0:8, :] = pa_s[tt:tt + 8, :]
    r = pa[:, 0:D_MIX]
    k = pa[:, D_MIX:2 * D_MIX]
    v = pa[:, 2 * D_MIX:3 * D_MIX]
    lora = pa[:, 3 * D_MIX:A_COLS]

    def lora_w(w_ref, start):
        w = w_ref[0]
        parts = []
        if start:
            parts.append(jnp.zeros((start, D_MIX), F32))
        parts.append(w)
        if LORA_COLS - start - w.shape[0]:
            parts.append(jnp.zeros((LORA_COLS - start - w.shape[0], D_MIX), F32))
        return jnp.concatenate(parts, axis=0).astype(BF16)

    lw = -math.exp(-0.5) * _sigmoid(prow(ROW_W0) + _dot(jnp.tanh(lora), lora_w(w2_ref, 0)))
    if has_vmix:
        vgate = _sigmoid(v0_ref[l - 1:l, :] + _dot(_dot(v, v1_ref[0]), v2_ref[0]))
        v = v + (vf_ref[0] - v) * vgate
    else:
        vout_ref[0] = v
    a = _sigmoid(prow(ROW_A0) + _dot(lora, lora_w(a2_ref, LORA_W)))
    g = _dot(_sigmoid(lora), lora_w(g2_ref, LORA_W + LORA_A))
    kk = k * prow(ROW_KK)
    seg = seg_ref[...]
    kk_norm = jnp.sqrt(_dot_x_sel(kk * kk, seg))
    kk = kk / jnp.maximum(kk_norm, 1e-12)
    k = k * (1.0 + (a - 1.0) * prow(ROW_KA))
    bonus = _dot_x_sel(r * k * prow(ROW_RK), seg) * v
    rw_s[:, 0 * D_MIX:1 * D_MIX] = r
    rw_s[:, 1 * D_MIX:2 * D_MIX] = lw
    rw_s[:, 2 * D_MIX:3 * D_MIX] = k
    rw_s[:, 3 * D_MIX:4 * D_MIX] = v
    rw_s[:, 4 * D_MIX:5 * D_MIX] = kk
    rw_s[:, 5 * D_MIX:6 * D_MIX] = kk * a
    rw_s[:, 6 * D_MIX:7 * D_MIX] = g
    rw_s[:, 7 * D_MIX:8 * D_MIX] = bonus

    z = _gelu_tanh(p[:, OFF_B:OFF_C])
    su = z[:, :D_MIX]
    sv = z[:, D_MIX:]
    mu_v = jnp.mean(sv, axis=-1, keepdims=True)
    var_v = jnp.mean(jnp.square(sv - mu_v), axis=-1, keepdims=True)
    sv = (sv - mu_v) * lax.rsqrt(var_v + LN_EPS) * prow(ROW_SGW) + prow(ROW_SGB)
    row = lax.broadcasted_iota(jnp.int32, (CHUNK, CHUNK), 0)
    col = lax.broadcasted_iota(jnp.int32, (CHUNK, CHUNK), 1)
    causal = row >= col
    wsm = [jnp.where(causal, ws_ref[0, gi], 0.0).astype(BF16) for gi in range(G_B)]
    sv_b = sv.astype(BF16)
    for n in range(tt // CHUNK):
        rows = slice(n * CHUNK, (n + 1) * CHUNK)
        mixed = jnp.concatenate(
            [jnp.dot(wsm[gi], sv_b[rows, gi * GROUP_B:(gi + 1) * GROUP_B], preferred_element_type=F32)
             for gi in range(G_B)], axis=1) + sgb_ref[0]
        fb_ref[0, rows, :] = (su[rows, :] * mixed).astype(BF16)

    pc = p[:, OFF_C:OFF_D]
    bg = pc[:, :D_MIX]
    zc = pc[:, D_MIX:2 * D_MIX] * pc[:, 2 * D_MIX:]
    z_s[8:8 + tt, :] = zc
    y = (cw_ref[l, 0:1, :] * z_s[6:6 + tt, :] + cw_ref[l, 1:2, :] * z_s[7:7 + tt, :]
         + cw_ref[l, 2:3, :] * zc)
    z_s[0:8, :] = z_s[tt:tt + 8, :]
    fc_ref[0] = (bg * y).astype(BF16)

    u5 = p[:, OFF_D:OFF_G]

    _wkv_part(tt, rw_s, prow(ROW_LNXW), prow(ROW_LNXB), fa_ref, s_ref)

    _s5_part(u5, perm_ref[...], bblk_ref[0], cre_ref, cim_ref, ab_ref[0, 0:1, :], ab_ref[0, 1:2, :],
             prow(ROW_S5D), fd_ref, bu_s, x_s, carry_s)


def _mix_call(l, x, mod4, ng, win, mu, rows, w2, a2, g2, seg, ws, sgb, cw, perm, bblk, cre, cim, ab,
              v0, v1, v2, v_first, tt):
    bsz, seq, d = x.shape
    has_vmix = l > 0
    ins = [x, mod4, ng, win, mu, rows, w2, a2, g2, seg, ws, sgb, cw, perm, bblk, cre, cim, ab]
    in_specs = [
        _tok_spec(tt, d), _mod_spec(mod4, l), _const_spec(ng),
        pl.BlockSpec((1, d, OFF_G), lambda b, t: (l, 0, 0)),
        _const_spec(mu), _const_spec(rows), _layer_spec(w2, l), _layer_spec(a2, l), _layer_spec(g2, l),
        _const_spec(seg), _layer_spec(ws, l), _layer_spec(sgb, l), _const_spec(cw), _const_spec(perm),
        _layer_spec(bblk, l), _layer_spec(cre, l), _layer_spec(cim, l), _layer_spec(ab, l),
    ]
    feat = jax.ShapeDtypeStruct((bsz, seq, D_MIX), BF16)
    out_shape = [feat, feat, feat, feat]
    out_specs = [_tok_spec(tt, D_MIX)] * 4
    if has_vmix:
        ins += [v_first, v0, v1, v2]
        in_specs += [_tok_spec(tt, D_MIX), _const_spec(v0), _layer_spec(v1, l - 1), _layer_spec(v2, l - 1)]
    else:
        out_shape.append(jax.ShapeDtypeStruct((bsz, seq, D_MIX), F32))
        out_specs.append(_tok_spec(tt, D_MIX))
    return pl.pallas_call(
        functools.partial(_mix_kernel, l, tt),
        grid=(bsz, seq // tt),
        in_specs=in_specs,
        out_specs=out_specs,
        out_shape=out_shape,
        scratch_shapes=[pltpu.VMEM((tt + 8, A_COLS), F32), pltpu.VMEM((tt + 8, D_MIX), F32),
                        pltpu.VMEM((tt, RW_COLS), F32), pltpu.VMEM((H_A, HEAD_A, HEAD_A), F32),
                        pltpu.VMEM((tt // S5_TILE, S5_TILE, S5_LANES), F32),
                        pltpu.VMEM((tt // S5_TILE, S5_TILE, S5_LANES), F32),
                        pltpu.VMEM((8, S5_STATES), F32)],
        compiler_params=_cparams(("parallel", "arbitrary")),
        name="mix",
    )(*ins)


def _s5_param_kernel(are_ref, aim_ref, ldt_ref, bre_ref, bim_ref, abre_ref, abim_ref, bbre_ref, bbim_ref):
    lam_re = jnp.minimum(are_ref[...], -1e-4)
    lam_im = aim_ref[...]
    dt = jnp.exp(ldt_ref[...])
    mag = jnp.exp(lam_re * dt)
    ab_re = mag * jnp.cos(lam_im * dt)
    ab_im = mag * jnp.sin(lam_im * dt)
    den = lam_re * lam_re + lam_im * lam_im
    q_re = ((ab_re - 1.0) * lam_re + ab_im * lam_im) / den
    q_im = (ab_im * lam_re - (ab_re - 1.0) * lam_im) / den
    abre_ref[...] = ab_re
    abim_ref[...] = ab_im
    b_re = bre_ref[...]
    b_im = bim_ref[...]
    bbre_ref[...] = q_re * b_re - q_im * b_im
    bbim_ref[...] = q_re * b_im + q_im * b_re


def _s5_param_call(a_re, a_im, log_dt, b_re, b_im):
    rows = a_re.size
    col = lambda a: a.reshape(rows, 1).astype(F32)
    ldt = jnp.repeat(log_dt.astype(F32), N_STATE, axis=-1).reshape(rows, 1)
    return pl.pallas_call(
        _s5_param_kernel,
        out_shape=(jax.ShapeDtypeStruct((rows, 1), F32), jax.ShapeDtypeStruct((rows, 1), F32),
                   jax.ShapeDtypeStruct((rows, GROUP_D), F32), jax.ShapeDtypeStruct((rows, GROUP_D), F32)),
        name="s5_params",
    )(col(a_re), col(a_im), ldt, b_re.reshape(rows, GROUP_D).astype(F32),
      b_im.reshape(rows, GROUP_D).astype(F32))


def _merge_kernel(l, x_ref, mod_ref, ng_ref, fa_ref, fb_ref, fc_ref, fd_ref, wg_ref, wa_ref, wb_ref, wc_ref,
                  glu_ref, wo_ref, o_ref):
    d = D_MODEL
    mod = mod_ref[0, 0]
    x = x_ref[0]
    hb = _rms_mod(x, ng_ref[l:l + 1, :], mod[1:2, :], mod[0:1, :]).astype(BF16)

    def gate(i):
        cols = slice(OFF_G + i * d, OFF_G + (i + 1) * d)
        return _sigmoid(jnp.dot(hb, wg_ref[0, :, cols], preferred_element_type=F32))

    merged = gate(0) * jnp.dot(fa_ref[0], wa_ref[0], preferred_element_type=F32)
    merged += gate(1) * jnp.dot(fb_ref[0], wb_ref[0], preferred_element_type=F32)
    merged += gate(2) * jnp.dot(fc_ref[0], wc_ref[0], preferred_element_type=F32)
    hd = jnp.dot(fd_ref[0], glu_ref[0], preferred_element_type=F32)
    merged += gate(3) * (hd[:, :d] * _sigmoid(hd[:, d:]))
    o_ref[0] = x + mod[2:3, :] * jnp.dot(merged.astype(BF16), wo_ref[0], preferred_element_type=F32)


def _merge_call(l, x, mod4, ng, fa, fb, fc, fd, wg, wa, wb, wc, glu, wo, tm):
    bsz, seq, d = x.shape
    feat = _tok_spec(tm, D_MIX)
    return pl.pallas_call(
        functools.partial(_merge_kernel, l),
        grid=(bsz, seq // tm),
        in_specs=[_tok_spec(tm, d), _mod_spec(mod4, l), _const_spec(ng), feat, feat, feat, feat,
                  _weight_spec(wg, l), _layer_spec(wa, l), _layer_spec(wb, l), _layer_spec(wc, l),
                  _weight_spec(glu, l), _weight_spec(wo, l)],
        out_specs=_tok_spec(tm, d),
        out_shape=jax.ShapeDtypeStruct((bsz, seq, d), F32),
        compiler_params=_cparams(("parallel", "parallel")),
        name="merge",
    )(x, mod4, ng, fa, fb, fc, fd, wg, wa, wb, wc, glu, wo)


def _ffn_kernel(l, final, x_ref, mod_ref, ng_ref, w1_ref, w2_ref, fg_ref, o_ref):
    mod = mod_ref[0, 0]
    x = x_ref[0]
    hb = _rms_mod(x, ng_ref[l:l + 1, :], mod[4:5, :], mod[3:4, :]).astype(BF16)
    acc = jnp.zeros(x.shape, F32)
    step = D_MODEL
    for j in range(D_FF // step):
        a = jnp.dot(hb, w1_ref[0, :, j * step:(j + 1) * step], preferred_element_type=F32)
        a = jnp.square(jnp.maximum(a, 0.0))
        acc += jnp.dot(a.astype(BF16), w2_ref[0, j * step:(j + 1) * step, :], preferred_element_type=F32)
    y = x + mod[5:6, :] * acc
    if final:
        ms = jnp.mean(y * y, axis=-1, keepdims=True)
        y = y * lax.rsqrt(ms + EPS) * fg_ref[...]
    o_ref[0] = y


def _ffn_call(l, x, mod4, ng, w1, w2, fg, final, tm):
    bsz, seq, d = x.shape
    return pl.pallas_call(
        functools.partial(_ffn_kernel, l, final),
        grid=(bsz, seq // tm),
        in_specs=[_tok_spec(tm, d), _mod_spec(mod4, l), _const_spec(ng), _weight_spec(w1, l),
                  _weight_spec(w2, l), _const_spec(fg)],
        out_specs=_tok_spec(tm, d),
        out_shape=jax.ShapeDtypeStruct((bsz, seq, d), F32),
        compiler_params=_cparams(("parallel", "parallel")),
        name="ffn",
    )(x, mod4, ng, w1, w2, fg)


def _tile(seq, want):
    return want if seq % want == 0 else seq


def kernel(x, c, ada_w, ada_b, norm_mix_g, w_in, rwkv_mu, rwkv_w0, rwkv_w2, rwkv_a0, rwkv_a2, rwkv_g2,
           rwkv_v0, rwkv_v1, rwkv_v2, rwkv_kk, rwkv_ka, rwkv_rk, rwkv_lnx_w, rwkv_lnx_b, rwkv_out,
           sg_ln_w, sg_ln_b, sg_ws, sg_bs, sg_out, conv_w, conv_out, s5_a_re, s5_a_im, s5_b_re, s5_b_im,
           s5_c_re, s5_c_im, s5_d, s5_log_dt, s5_glu_w, w_o, norm_ffn_g, ffn_w1, ffn_w2, final_g):
    in_dtype = x.dtype
    bsz, seq, d = x.shape
    depth = ada_w.shape[0]
    x = x.astype(F32)

    tt = _tile(seq, 512)
    tm = _tile(seq, 512)
    tm_ffn = _tile(seq, 1024)

    c_rows = 16
    c_pad = jnp.pad(c.astype(F32), ((0, c_rows - bsz), (0, 0)))
    mod4 = _ada_call(c_pad, ada_w.astype(F32), ada_b.astype(F32)).reshape(depth, c_rows, 6, d)

    rows = jnp.stack([rwkv_w0, rwkv_a0, rwkv_kk, rwkv_ka, rwkv_rk.reshape(depth, D_MIX), rwkv_lnx_w,
                      rwkv_lnx_b, sg_ln_w, sg_ln_b, s5_d], axis=0).astype(F32)
    sgb = jnp.repeat(jnp.swapaxes(sg_bs, 1, 2), GROUP_B, axis=2).astype(F32)
    win = w_in.astype(BF16)
    wa, wb, wc = rwkv_out.astype(BF16), sg_out.astype(BF16), conv_out.astype(BF16)
    glu = s5_glu_w.astype(BF16)
    wo = w_o.astype(BF16)
    w1 = ffn_w1.astype(BF16)
    w2 = ffn_w2.astype(BF16)
    fg = final_g.reshape(1, d).astype(F32)

    head_id = np.arange(D_MIX) // HEAD_A
    seg = jnp.asarray(head_id[:, None] == head_id[None, :], BF16)
    steps = S5_TILE // S5_SUBSEQ
    dst = np.arange(S5_TILE)
    src = (dst % S5_SUBSEQ) * steps + dst // S5_SUBSEQ
    perm = jnp.asarray(src[:, None] == np.arange(S5_TILE)[None, :], BF16)

    abre, abim, bbre, bbim = _s5_param_call(s5_a_re, s5_a_im, s5_log_dt, s5_b_re, s5_b_im)
    ab = jnp.concatenate([abre.reshape(depth, 1, S5_STATES), abim.reshape(depth, 1, S5_STATES)], axis=1)
    eye_g = jnp.asarray(np.eye(G_D), F32)

    def in_blk(bb):
        t = bb.reshape(depth, G_D, N_STATE, GROUP_D)
        return jnp.einsum('lgnc,gh->lgchn', t, eye_g).reshape(depth, D_MIX, S5_STATES)

    def out_blk(cc):
        return jnp.einsum('lgcn,gh->lgnhc', cc.astype(F32), eye_g).reshape(depth, S5_STATES, D_MIX).astype(BF16)

    bblk = jnp.concatenate([in_blk(bbre), in_blk(bbim)], axis=2).astype(BF16)
    cre = out_blk(s5_c_re)
    cim = out_blk(s5_c_im)

    f32 = lambda a: a.astype(F32)
    v_first = None
    for l in range(depth):
        outs = _mix_call(l, x, mod4, f32(norm_mix_g), win, f32(rwkv_mu), rows, f32(rwkv_w2), f32(rwkv_a2),
                         f32(rwkv_g2), seg, f32(sg_ws), sgb, f32(conv_w), perm, bblk, cre, cim, ab,
                         f32(rwkv_v0), f32(rwkv_v1), f32(rwkv_v2), v_first, tt)
        fa, fb, fc, fd = outs[:4]
        if l == 0:
            v_first = outs[4]
        x = _merge_call(l, x, mod4, f32(norm_mix_g), fa, fb, fc, fd, win, wa, wb, wc, glu, wo, tm)
        x = _ffn_call(l, x, mod4, f32(norm_ffn_g), w1, w2, fg, l == depth - 1, tm_ffn)
    return x.astype(in_dtype)
```

```python
import functools
import math

import numpy as np
import jax
import jax.numpy as jnp
from jax import lax
from jax.experimental import pallas as pl
from jax.experimental.pallas import tpu as pltpu

F32 = jnp.float32
BF16 = jnp.bfloat16

D_MODEL = 1024
N_BRANCH = 4
D_MIX = D_MODEL // N_BRANCH
HEAD_A = 64
H_A = D_MIX // HEAD_A
LORA_W = 32
LORA_A = 32
LORA_G = 64
LNX_EPS = 64e-5
CHUNK = 128
GROUP_B = 64
G_B = D_MIX // GROUP_B
CONV_K = 3
GROUP_D = 16
G_D = D_MIX // GROUP_D
N_STATE = 64
D_FF = 4 * D_MODEL
EPS = 1e-6
LN_EPS = 1e-5

A_COLS = 3 * D_MIX + LORA_W + LORA_A + LORA_G
B_COLS = 2 * D_MIX
C_COLS = 3 * D_MIX
D_COLS = D_MIX
OFF_B = A_COLS
OFF_C = OFF_B + B_COLS
OFF_D = OFF_C + C_COLS
OFF_G = OFF_D + D_COLS
LORA_COLS = LORA_W + LORA_A + LORA_G

WKV_CHUNK = 64
S5_STATES = G_D * N_STATE
S5_LANES = 2 * S5_STATES
S5_SUBSEQ = 8
S5_TILE = 256
FRONT_ROWS = 256
RW_COLS = 8 * D_MIX

(ROW_W0, ROW_A0, ROW_KK, ROW_KA, ROW_RK, ROW_LNXW, ROW_LNXB, ROW_SGW, ROW_SGB, ROW_S5D) = range(10)

VMEM_LIMIT = 56 * 1024 * 1024


def _cparams(sem):
    return pltpu.CompilerParams(dimension_semantics=sem, vmem_limit_bytes=VMEM_LIMIT)


def _dot(a, b):
    return jnp.dot(a.astype(BF16), b.astype(BF16), preferred_element_type=F32)


def _dot_nt(a, b):
    return lax.dot_general(a.astype(BF16), b.astype(BF16), (((1,), (1,)), ((), ())),
                           preferred_element_type=F32)


def _dot_tn(a, b):
    return lax.dot_general(a.astype(BF16), b.astype(BF16), (((0,), (0,)), ((), ())),
                           preferred_element_type=F32)


def _split3(x):
    hi = x.astype(BF16)
    r1 = x - hi.astype(F32)
    mid = r1.astype(BF16)
    lo = (r1 - mid.astype(F32)).astype(BF16)
    return hi, mid, lo


def _split2(x):
    hi = x.astype(BF16)
    return hi, (x - hi.astype(F32)).astype(BF16)


def _dot_sel(sel, x):
    hi, mid = _split2(x)
    return jnp.dot(sel, hi, preferred_element_type=F32) + jnp.dot(sel, mid, preferred_element_type=F32)


def _dot_x_sel(x, sel):
    hi, mid = _split2(x)
    return jnp.dot(hi, sel, preferred_element_type=F32) + jnp.dot(mid, sel, preferred_element_type=F32)


def _dot3(a, b):
    ah, am, _ = _split3(a)
    bh, bm, _ = _split3(b)
    return (jnp.dot(ah, bh, preferred_element_type=F32) + jnp.dot(ah, bm, preferred_element_type=F32)
            + jnp.dot(am, bh, preferred_element_type=F32))


def _sigmoid(x):
    return 0.5 * jnp.tanh(0.5 * x) + 0.5


def _gelu_tanh(x):
    return 0.5 * x * (1.0 + jnp.tanh(math.sqrt(2.0 / math.pi) * (x + 0.044715 * (x * x * x))))


def _rms_mod(x, g, sc, sh):
    ms = jnp.mean(x * x, axis=-1, keepdims=True)
    return x * lax.rsqrt(ms + EPS) * g * (1.0 + sc) + sh


def _layer_spec(a, l):
    return pl.BlockSpec((1,) + a.shape[1:], lambda b, t: (l,) + (0,) * (a.ndim - 1))


def _weight_spec(a, l):
    return pl.BlockSpec((1,) + a.shape[1:], lambda b, t: (l,) + (0,) * (a.ndim - 1),
                        pipeline_mode=pl.Buffered(1))


def _mod_spec(mod4, l):
    return pl.BlockSpec((1, 1) + mod4.shape[2:], lambda b, t: (l, b, 0, 0))


def _const_spec(a):
    return pl.BlockSpec(a.shape, lambda b, t: (0,) * a.ndim)


def _tok_spec(tt, cols):
    return pl.BlockSpec((1, tt, cols), lambda b, t: (b, t, 0))


def _ada_kernel(c_ref, w_ref, b_ref, o_ref):
    c = c_ref[...]
    ca = c * _sigmoid(c)
    o_ref[0] = _dot3(ca, w_ref[0]) + b_ref[0]


def _ada_call(c_pad, ada_w, ada_b):
    depth, d, n = ada_w.shape
    rows = c_pad.shape[0]
    bn = 1536
    return pl.pallas_call(
        _ada_kernel,
        grid=(depth, n // bn),
        in_specs=[
            pl.BlockSpec((rows, d), lambda l, j: (0, 0)),
            pl.BlockSpec((1, d, bn), lambda l, j: (l, 0, j)),
            pl.BlockSpec((1, 1, bn), lambda l, j: (l, 0, j)),
        ],
        out_specs=pl.BlockSpec((1, rows, bn), lambda l, j: (l, 0, j)),
        out_shape=jax.ShapeDtypeStruct((depth, rows, n), F32),
        compiler_params=_cparams(("parallel", "parallel")),
        name="ada_mod",
    )(c_pad, ada_w, ada_b.reshape(depth, 1, n))


def _wkv_part(tt, rw_s, lnx_w, lnx_b, o_ref, s_ref):
    n = WKV_CHUNK
    nc = tt // n

    row = lax.broadcasted_iota(jnp.int32, (n, n), 0)
    col = lax.broadcasted_iota(jnp.int32, (n, n), 1)
    ltri = jnp.where(row >= col, 1.0, 0.0).astype(BF16)
    strict = row > col
    incl = row >= col
    eye = jnp.where(row == col, 1.0, 0.0)
    same = lambda s: jnp.right_shift(row, s) == jnp.right_shift(col, s)

    items = [(ci, hd) for ci in range(nc) for hd in range(H_A)]
    head = lambda x, hd: x[:, hd * HEAD_A:(hd + 1) * HEAD_A]

    a_t, r_t, b_t, k_t, b_h, k_h, vv, g_end = [], [], [], [], [], [], [], []
    for ci in range(nc):
        rows = slice(ci * n, (ci + 1) * n)
        r = rw_s[rows, 0 * D_MIX:1 * D_MIX]
        lw = rw_s[rows, 1 * D_MIX:2 * D_MIX]
        k = rw_s[rows, 2 * D_MIX:3 * D_MIX]
        kk = rw_s[rows, 4 * D_MIX:5 * D_MIX]
        b = rw_s[rows, 5 * D_MIX:6 * D_MIX]
        e = _dot_sel(ltri, lw)
        eg = jnp.exp(e)
        ig = jnp.exp(-e)
        ge = eg[n - 1:n, :]
        a_t.append((-kk * jnp.exp(e - lw)).astype(BF16))
        r_t.append(r * eg)
        b_t.append((b * ig).astype(BF16))
        k_t.append((k * ig).astype(BF16))
        b_h.append((b * ig * ge).astype(BF16))
        k_h.append((k * ig * ge).astype(BF16))
        vv.append(rw_s[rows, 3 * D_MIX:4 * D_MIX].astype(BF16))
        g_end.append(ge)

    gm = [_dot_nt(jnp.concatenate([head(a_t[ci], hd), head(r_t[ci], hd).astype(BF16)], axis=0),
                  jnp.concatenate([head(b_t[ci], hd), head(k_t[ci], hd)], axis=0)) for ci, hd in items]
    a_ab = [jnp.where(strict, m[:n, :n], 0.0) for m in gm]
    a_ak = [jnp.where(strict, m[:n, n:], 0.0) for m in gm]
    a_rb = [jnp.where(incl, m[n:, :n], 0.0) for m in gm]
    a_rk = [jnp.where(incl, m[n:, n:], 0.0) for m in gm]
    inv = [eye + jnp.where(same(1), a, 0.0) for a in a_ab]
    for s in range(2, int(math.log2(n)) + 1):
        off_mask = same(s) & jnp.logical_not(same(s - 1))
        tmp = [_dot(t, jnp.where(off_mask, a, 0.0)) for t, a in zip(inv, a_ab)]
        inv = [t + _dot(x, t) for t, x in zip(inv, tmp)]
    akv = [_dot(a_ak[i], head(vv[ci], hd)) for i, (ci, hd) in enumerate(items)]
    pq = [_dot(inv[i], jnp.concatenate([head(a_t[ci], hd), akv[i].astype(BF16)], axis=1))
          for i, (ci, hd) in enumerate(items)]
    mn = [_dot_tn(pq[i], head(b_h[ci], hd)) for i, (ci, hd) in enumerate(items)]
    kv = [_dot_tn(head(vv[ci], hd), head(k_h[ci], hd)) for ci, hd in items]

    state = [s_ref[hd] for hd in range(H_A)]
    starts, rq, ark = [], [], []
    for ci in range(nc):
        starts.append(list(state))
        for hd in range(H_A):
            i = ci * H_A + hd
            s0 = state[hd]
            state[hd] = s0 * head(g_end[ci], hd) + _dot(s0, mn[i][:n, :]) + (mn[i][n:, :] + kv[i])
        for hd in range(H_A):
            i = ci * H_A + hd
            rq.append(_dot(a_rb[i], pq[i]))
            ark.append(_dot(a_rk[i], head(vv[ci], hd)))
    for hd in range(H_A):
        s_ref[hd] = state[hd]

    for ci in range(nc):
        outs = []
        for hd in range(H_A):
            i = ci * H_A + hd
            s0 = starts[ci][hd]
            ro = head(r_t[ci], hd) + rq[i][:, :n]
            o = _dot_nt(ro, s0) + rq[i][:, n:] + ark[i]
            mu = jnp.mean(o, axis=-1, keepdims=True)
            var = jnp.mean(jnp.square(o - mu), axis=-1, keepdims=True)
            outs.append((o - mu) * lax.rsqrt(var + LNX_EPS))
        rows = slice(ci * n, (ci + 1) * n)
        on = jnp.concatenate(outs, axis=1) * lnx_w + lnx_b
        g = rw_s[rows, 6 * D_MIX:7 * D_MIX]
        bonus = rw_s[rows, 7 * D_MIX:8 * D_MIX]
        o_ref[0, rows, :] = ((on + bonus) * g).astype(BF16)


def _s5_part(u5, perm, bblk, cre_ref, cim_ref, ab_re, ab_im, d_row, o_ref, bu_s, x_s, carry_s):
    half = S5_STATES
    steps = S5_TILE // S5_SUBSEQ
    nt = u5.shape[0] // S5_TILE

    u_p = [_dot_sel(perm, u5[n * S5_TILE:(n + 1) * S5_TILE, :]) for n in range(nt)]
    for n in range(nt):
        bu_s[n] = jnp.dot(u_p[n].astype(BF16), bblk, preferred_element_type=F32)

    a_re = jnp.broadcast_to(ab_re, (S5_SUBSEQ, half))
    a_im = jnp.broadcast_to(ab_im, (S5_SUBSEQ, half))

    def step(n, i, st):
        s_re, s_im = st
        rows = slice(i * S5_SUBSEQ, (i + 1) * S5_SUBSEQ)
        n_re = a_re * s_re - a_im * s_im + bu_s[n, rows, 0:half]
        n_im = a_re * s_im + a_im * s_re + bu_s[n, rows, half:]
        return n_re, n_im

    zero = jnp.zeros((S5_SUBSEQ, half), F32)
    st = [(zero, zero)] * nt
    for i in range(steps):
        st = [step(n, i, st[n]) for n in range(nt)]
    ends = st

    p_re, p_im = ab_re, ab_im
    for _ in range(int(math.log2(steps))):
        p_re, p_im = p_re * p_re - p_im * p_im, 2.0 * p_re * p_im

    c_re, c_im = carry_s[0:1, :], carry_s[1:2, :]
    st = []
    for n in range(nt):
        e_re, e_im = ends[n]
        in_re, in_im = [], []
        for j in range(S5_SUBSEQ):
            in_re.append(c_re)
            in_im.append(c_im)
            c_re, c_im = (p_re * c_re - p_im * c_im + e_re[j:j + 1, :],
                          p_re * c_im + p_im * c_re + e_im[j:j + 1, :])
        st.append((jnp.concatenate(in_re, axis=0), jnp.concatenate(in_im, axis=0)))
    carry_s[0:1, :] = c_re
    carry_s[1:2, :] = c_im

    for i in range(steps):
        st = [step(n, i, st[n]) for n in range(nt)]
        rows = slice(i * S5_SUBSEQ, (i + 1) * S5_SUBSEQ)
        for n in range(nt):
            x_s[n, rows, 0:half] = st[n][0]
            x_s[n, rows, half:] = st[n][1]

    for n in range(nt):
        y = (jnp.dot(x_s[n, :, 0:half].astype(BF16), cre_ref[0], preferred_element_type=F32)
             - jnp.dot(x_s[n, :, half:].astype(BF16), cim_ref[0], preferred_element_type=F32))
        f_p = _gelu_tanh(y + d_row * u_p[n]).astype(BF16)
        o_ref[0, n * S5_TILE:(n + 1) * S5_TILE, :] = lax.dot_general(
            perm, f_p, (((0,), (0,)), ((), ())), preferred_element_type=F32).astype(BF16)


def _mix_kernel(l, tt, *refs):
    has_vmix = l > 0
    refs = list(refs)
    (x_ref, mod_ref, ng_ref, win_ref, mu_ref, rows_ref, w2_ref, a2_ref, g2_ref, seg_ref, ws_ref, sgb_ref,
     cw_ref, perm_ref, bblk_ref, cre_ref, cim_ref, ab_ref) = refs[:18]
    refs = refs[18:]
    if has_vmix:
        vf_ref, v0_ref, v1_ref, v2_ref = refs[:4]
        refs = refs[4:]
        fa_ref, fb_ref, fc_ref, fd_ref = refs[:4]
        refs = refs[4:]
    else:
        fa_ref, fb_ref, fc_ref, fd_ref, vout_ref = refs[:5]
        refs = refs[5:]
    pa_s, z_s, rw_s, s_ref, bu_s, x_s, carry_s = refs
    prow = lambda i: rows_ref[i, l:l + 1, :]

    @pl.when(pl.program_id(1) == 0)
    def _():
        pa_s[0:8, :] = jnp.zeros((8, A_COLS), F32)
        z_s[0:8, :] = jnp.zeros((8, D_MIX), F32)
        s_ref[...] = jnp.zeros(s_ref.shape, F32)
        carry_s[...] = jnp.zeros(carry_s.shape, F32)

    mod = mod_ref[0, 0]

    def lora_w(w_ref, start):
        w = w_ref[0]
        parts = []
        if start:
            parts.append(jnp.zeros((start, D_MIX), F32))
        parts.append(w)
        if LORA_COLS - start - w.shape[0]:
            parts.append(jnp.zeros((LORA_COLS - start - w.shape[0], D_MIX), F32))
        return jnp.concatenate(parts, axis=0).astype(BF16)

    w2p, a2p, g2p = lora_w(w2_ref, 0), lora_w(a2_ref, LORA_W), lora_w(g2_ref, LORA_W + LORA_A)
    seg = seg_ref[...]
    row = lax.broadcasted_iota(jnp.int32, (CHUNK, CHUNK), 0)
    col = lax.broadcasted_iota(jnp.int32, (CHUNK, CHUNK), 1)
    wsm = [jnp.where(row >= col, ws_ref[0, gi], 0.0).astype(BF16) for gi in range(G_B)]

    u5_blocks = []
    for r0 in range(0, tt, FRONT_ROWS):
        rb = slice(r0, r0 + FRONT_ROWS)
        h = _rms_mod(x_ref[0, rb, :], ng_ref[l:l + 1, :], mod[1:2, :], mod[0:1, :])
        p = jnp.dot(h.astype(BF16), win_ref[0], preferred_element_type=F32)

        pa = p[:, :A_COLS]
        pa_s[8 + r0:8 + r0 + FRONT_ROWS, :] = pa
        prev = pa_s[7 + r0:7 + r0 + FRONT_ROWS, :]
        pa = pa + (prev - pa) * mu_ref[l:l + 1, :]
        r = pa[:, 0:D_MIX]
        k = pa[:, D_MIX:2 * D_MIX]
        v = pa[:, 2 * D_MIX:3 * D_MIX]
        lora = pa[:, 3 * D_MIX:A_COLS]
        lw = -math.exp(-0.5) * _sigmoid(prow(ROW_W0) + _dot(jnp.tanh(lora), w2p))
        if has_vmix:
            vgate = _sigmoid(v0_ref[l - 1:l, :] + _dot(_dot(v, v1_ref[0]), v2_ref[0]))
            v = v + (vf_ref[0, rb, :] - v) * vgate
        else:
            vout_ref[0, rb, :] = v
        a = _sigmoid(prow(ROW_A0) + _dot(lora, a2p))
        g = _dot(_sigmoid(lora), g2p)
        kk = k * prow(ROW_KK)
        kk_norm = jnp.sqrt(_dot_x_sel(kk * kk, seg))
        kk = kk / jnp.maximum(kk_norm, 1e-12)
        k = k * (1.0 + (a - 1.0) * prow(ROW_KA))
        bonus = _dot_x_sel(r * k * prow(ROW_RK), seg) * v
        rw_s[rb, 0 * D_MIX:1 * D_MIX] = r
        rw_s[rb, 1 * D_MIX:2 * D_MIX] = lw
        rw_s[rb, 2 * D_MIX:3 * D_MIX] = k
        rw_s[rb, 3 * D_MIX:4 * D_MIX] = v
        rw_s[rb, 4 * D_MIX:5 * D_MIX] = kk
        rw_s[rb, 5 * D_MIX:6 * D_MIX] = kk * a
        rw_s[rb, 6 * D_MIX:7 * D_MIX] = g
        rw_s[rb, 7 * D_MIX:8 * D_MIX] = bonus

        z = _gelu_tanh(p[:, OFF_B:OFF_C])
        su = z[:, :D_MIX]
        sv = z[:, D_MIX:]
        mu_v = jnp.mean(sv, axis=-1, keepdims=True)
        var_v = jnp.mean(jnp.square(sv - mu_v), axis=-1, keepdims=True)
        sv = (sv - mu_v) * lax.rsqrt(var_v + LN_EPS) * prow(ROW_SGW) + prow(ROW_SGB)
        sv_b = sv.astype(BF16)
        for n in range(FRONT_ROWS // CHUNK):
            rows = slice(n * CHUNK, (n + 1) * CHUNK)
            mixed = jnp.concatenate(
                [jnp.dot(wsm[gi], sv_b[rows, gi * GROUP_B:(gi + 1) * GROUP_B], preferred_element_type=F32)
                 for gi in range(G_B)], axis=1) + sgb_ref[0]
            fb_ref[0, r0 + n * CHUNK:r0 + (n + 1) * CHUNK, :] = (su[rows, :] * mixed).astype(BF16)

        pc = p[:, OFF_C:OFF_D]
        bg = pc[:, :D_MIX]
        zc = pc[:, D_MIX:2 * D_MIX] * pc[:, 2 * D_MIX:]
        z_s[8 + r0:8 + r0 + FRONT_ROWS, :] = zc
        y = (cw_ref[l, 0:1, :] * z_s[6 + r0:6 + r0 + FRONT_ROWS, :]
             + cw_ref[l, 1:2, :] * z_s[7 + r0:7 + r0 + FRONT_ROWS, :] + cw_ref[l, 2:3, :] * zc)
        fc_ref[0, rb, :] = (bg * y).astype(BF16)

        u5_blocks.append(p[:, OFF_D:OFF_G])

    pa_s[0:8, :] = pa_s[tt:tt + 8, :]
    z_s[0:8, :] = z_s[tt:tt + 8, :]
    u5 = jnp.concatenate(u5_blocks, axis=0)

    _wkv_part(tt, rw_s, prow(ROW_LNXW), prow(ROW_LNXB), fa_ref, s_ref)

    _s5_part(u5, perm_ref[...], bblk_ref[0], cre_ref, cim_ref, ab_ref[0, 0:1, :], ab_ref[0, 1:2, :],
             prow(ROW_S5D), fd_ref, bu_s, x_s, carry_s)


def _mix_call(l, x, mod4, ng, win, mu, rows, w2, a2, g2, seg, ws, sgb, cw, perm, bblk, cre, cim, ab,
              v0, v1, v2, v_first, tt):
    bsz, seq, d = x.shape
    has_vmix = l > 0
    ins = [x, mod4, ng, win, mu, rows, w2, a2, g2, seg, ws, sgb, cw, perm, bblk, cre, cim, ab]
    in_specs = [
        _tok_spec(tt, d), _mod_spec(mod4, l), _const_spec(ng),
        pl.BlockSpec((1, d, OFF_G), lambda b, t: (l, 0, 0)),
        _const_spec(mu), _const_spec(rows), _layer_spec(w2, l), _layer_spec(a2, l), _layer_spec(g2, l),
        _const_spec(seg), _layer_spec(ws, l), _layer_spec(sgb, l), _const_spec(cw), _const_spec(perm),
        _layer_spec(bblk, l), _layer_spec(cre, l), _layer_spec(cim, l), _layer_spec(ab, l),
    ]
    feat = jax.ShapeDtypeStruct((bsz, seq, D_MIX), BF16)
    out_shape = [feat, feat, feat, feat]
    out_specs = [_tok_spec(tt, D_MIX)] * 4
    if has_vmix:
        ins += [v_first, v0, v1, v2]
        in_specs += [_tok_spec(tt, D_MIX), _const_spec(v0), _layer_spec(v1, l - 1), _layer_spec(v2, l - 1)]
    else:
        out_shape.append(jax.ShapeDtypeStruct((bsz, seq, D_MIX), F32))
        out_specs.append(_tok_spec(tt, D_MIX))
    return pl.pallas_call(
        functools.partial(_mix_kernel, l, tt),
        grid=(bsz, seq // tt),
        in_specs=in_specs,
        out_specs=out_specs,
        out_shape=out_shape,
        scratch_shapes=[pltpu.VMEM((tt + 8, A_COLS), F32), pltpu.VMEM((tt + 8, D_MIX), F32),
                        pltpu.VMEM((tt, RW_COLS), F32), pltpu.VMEM((H_A, HEAD_A, HEAD_A), F32),
                        pltpu.VMEM((tt // S5_TILE, S5_TILE, S5_LANES), F32),
                        pltpu.VMEM((tt // S5_TILE, S5_TILE, S5_LANES), F32),
                        pltpu.VMEM((8, S5_STATES), F32)],
        compiler_params=_cparams(("parallel", "arbitrary")),
        name="mix",
    )(*ins)


def _s5_param_kernel(are_ref, aim_ref, ldt_ref, bre_ref, bim_ref, abre_ref, abim_ref, bbre_ref, bbim_ref):
    lam_re = jnp.minimum(are_ref[...], -1e-4)
    lam_im = aim_ref[...]
    dt = jnp.exp(ldt_ref[...])
    mag = jnp.exp(lam_re * dt)
    ab_re = mag * jnp.cos(lam_im * dt)
    ab_im = mag * jnp.sin(lam_im * dt)
    den = lam_re * lam_re + lam_im * lam_im
    q_re = ((ab_re - 1.0) * lam_re + ab_im * lam_im) / den
    q_im = (ab_im * lam_re - (ab_re - 1.0) * lam_im) / den
    abre_ref[...] = ab_re
    abim_ref[...] = ab_im
    b_re = bre_ref[...]
    b_im = bim_ref[...]
    bbre_ref[...] = q_re * b_re - q_im * b_im
    bbim_ref[...] = q_re * b_im + q_im * b_re


def _s5_param_call(a_re, a_im, log_dt, b_re, b_im):
    rows = a_re.size
    col = lambda a: a.reshape(rows, 1).astype(F32)
    ldt = jnp.repeat(log_dt.astype(F32), N_STATE, axis=-1).reshape(rows, 1)
    return pl.pallas_call(
        _s5_param_kernel,
        out_shape=(jax.ShapeDtypeStruct((rows, 1), F32), jax.ShapeDtypeStruct((rows, 1), F32),
                   jax.ShapeDtypeStruct((rows, GROUP_D), F32), jax.ShapeDtypeStruct((rows, GROUP_D), F32)),
        name="s5_params",
    )(col(a_re), col(a_im), ldt, b_re.reshape(rows, GROUP_D).astype(F32),
      b_im.reshape(rows, GROUP_D).astype(F32))


def _merge_kernel(l, x_ref, mod_ref, ng_ref, fa_ref, fb_ref, fc_ref, fd_ref, wg_ref, wa_ref, wb_ref, wc_ref,
                  glu_ref, wo_ref, o_ref):
    d = D_MODEL
    mod = mod_ref[0, 0]
    x = x_ref[0]
    hb = _rms_mod(x, ng_ref[l:l + 1, :], mod[1:2, :], mod[0:1, :]).astype(BF16)

    def gate(i):
        cols = slice(OFF_G + i * d, OFF_G + (i + 1) * d)
        return _sigmoid(jnp.dot(hb, wg_ref[0, :, cols], preferred_element_type=F32))

    merged = gate(0) * jnp.dot(fa_ref[0], wa_ref[0], preferred_element_type=F32)
    merged += gate(1) * jnp.dot(fb_ref[0], wb_ref[0], preferred_element_type=F32)
    merged += gate(2) * jnp.dot(fc_ref[0], wc_ref[0], preferred_element_type=F32)
    hd = jnp.dot(fd_ref[0], glu_ref[0], preferred_element_type=F32)
    merged += gate(3) * (hd[:, :d] * _sigmoid(hd[:, d:]))
    o_ref[0] = x + mod[2:3, :] * jnp.dot(merged.astype(BF16), wo_ref[0], preferred_element_type=F32)


def _merge_call(l, x, mod4, ng, fa, fb, fc, fd, wg, wa, wb, wc, glu, wo, tm):
    bsz, seq, d = x.shape
    feat = _tok_spec(tm, D_MIX)
    return pl.pallas_call(
        functools.partial(_merge_kernel, l),
        grid=(bsz, seq // tm),
        in_specs=[_tok_spec(tm, d), _mod_spec(mod4, l), _const_spec(ng), feat, feat, feat, feat,
                  _weight_spec(wg, l), _layer_spec(wa, l), _layer_spec(wb, l), _layer_spec(wc, l),
                  _weight_spec(glu, l), _weight_spec(wo, l)],
        out_specs=_tok_spec(tm, d),
        out_shape=jax.ShapeDtypeStruct((bsz, seq, d), F32),
        compiler_params=_cparams(("parallel", "parallel")),
        name="merge",
    )(x, mod4, ng, fa, fb, fc, fd, wg, wa, wb, wc, glu, wo)


def _ffn_kernel(l, final, x_ref, mod_ref, ng_ref, w1_ref, w2_ref, fg_ref, o_ref):
    mod = mod_ref[0, 0]
    x = x_ref[0]
    hb = _rms_mod(x, ng_ref[l:l + 1, :], mod[4:5, :], mod[3:4, :]).astype(BF16)
    acc = jnp.zeros(x.shape, F32)
    step = D_MODEL
    for j in range(D_FF // step):
        a = jnp.dot(hb, w1_ref[0, :, j * step:(j + 1) * step], preferred_element_type=F32)
        a = jnp.square(jnp.maximum(a, 0.0))
        acc += jnp.dot(a.astype(BF16), w2_ref[0, j * step:(j + 1) * step, :], preferred_element_type=F32)
    y = x + mod[5:6, :] * acc
    if final:
        ms = jnp.mean(y * y, axis=-1, keepdims=True)
        y = y * lax.rsqrt(ms + EPS) * fg_ref[...]
    o_ref[0] = y


def _ffn_call(l, x, mod4, ng, w1, w2, fg, final, tm):
    bsz, seq, d = x.shape
    return pl.pallas_call(
        functools.partial(_ffn_kernel, l, final),
        grid=(bsz, seq // tm),
        in_specs=[_tok_spec(tm, d), _mod_spec(mod4, l), _const_spec(ng), _weight_spec(w1, l),
                  _weight_spec(w2, l), _const_spec(fg)],
        out_specs=_tok_spec(tm, d),
        out_shape=jax.ShapeDtypeStruct((bsz, seq, d), F32),
        compiler_params=_cparams(("parallel", "parallel")),
        name="ffn",
    )(x, mod4, ng, w1, w2, fg)


def _tile(seq, want):
    return want if seq % want == 0 else seq


def kernel(x, c, ada_w, ada_b, norm_mix_g, w_in, rwkv_mu, rwkv_w0, rwkv_w2, rwkv_a0, rwkv_a2, rwkv_g2,
           rwkv_v0, rwkv_v1, rwkv_v2, rwkv_kk, rwkv_ka, rwkv_rk, rwkv_lnx_w, rwkv_lnx_b, rwkv_out,
           sg_ln_w, sg_ln_b, sg_ws, sg_bs, sg_out, conv_w, conv_out, s5_a_re, s5_a_im, s5_b_re, s5_b_im,
           s5_c_re, s5_c_im, s5_d, s5_log_dt, s5_glu_w, w_o, norm_ffn_g, ffn_w1, ffn_w2, final_g):
    in_dtype = x.dtype
    bsz, seq, d = x.shape
    depth = ada_w.shape[0]
    x = x.astype(F32)

    tt = _tile(seq, 512)
    tm = _tile(seq, 512)
    tm_ffn = _tile(seq, 1024)

    c_rows = 16
    c_pad = jnp.pad(c.astype(F32), ((0, c_rows - bsz), (0, 0)))
    mod4 = _ada_call(c_pad, ada_w.astype(F32), ada_b.astype(F32)).reshape(depth, c_rows, 6, d)

    rows = jnp.stack([rwkv_w0, rwkv_a0, rwkv_kk, rwkv_ka, rwkv_rk.reshape(depth, D_MIX), rwkv_lnx_w,
                      rwkv_lnx_b, sg_ln_w, sg_ln_b, s5_d], axis=0).astype(F32)
    sgb = jnp.repeat(jnp.swapaxes(sg_bs, 1, 2), GROUP_B, axis=2).astype(F32)
    win = w_in.astype(BF16)
    wa, wb, wc = rwkv_out.astype(BF16), sg_out.astype(BF16), conv_out.astype(BF16)
    glu = s5_glu_w.astype(BF16)
    wo = w_o.astype(BF16)
    w1 = ffn_w1.astype(BF16)
    w2 = ffn_w2.astype(BF16)
    fg = final_g.reshape(1, d).astype(F32)

    head_id = np.arange(D_MIX) // HEAD_A
    seg = jnp.asarray(head_id[:, None] == head_id[None, :], BF16)
    steps = S5_TILE // S5_SUBSEQ
    dst = np.arange(S5_TILE)
    src = (dst % S5_SUBSEQ) * steps + dst // S5_SUBSEQ
    perm = jnp.asarray(src[:, None] == np.arange(S5_TILE)[None, :], BF16)

    abre, abim, bbre, bbim = _s5_param_call(s5_a_re, s5_a_im, s5_log_dt, s5_b_re, s5_b_im)
    ab = jnp.concatenate([abre.reshape(depth, 1, S5_STATES), abim.reshape(depth, 1, S5_STATES)], axis=1)
    eye_g = jnp.asarray(np.eye(G_D), F32)

    def in_blk(bb):
        t = bb.reshape(depth, G_D, N_STATE, GROUP_D)
        return jnp.einsum('lgnc,gh->lgchn', t, eye_g).reshape(depth, D_MIX, S5_STATES)

    def out_blk(cc):
        return jnp.einsum('lgcn,gh->lgnhc', cc.astype(F32), eye_g).reshape(depth, S5_STATES, D_MIX).astype(BF16)

    bblk = jnp.concatenate([in_blk(bbre), in_blk(bbim)], axis=2).astype(BF16)
    cre = out_blk(s5_c_re)
    cim = out_blk(s5_c_im)

    f32 = lambda a: a.astype(F32)
    v_first = None
    for l in range(depth):
        outs = _mix_call(l, x, mod4, f32(norm_mix_g), win, f32(rwkv_mu), rows, f32(rwkv_w2), f32(rwkv_a2),
                         f32(rwkv_g2), seg, f32(sg_ws), sgb, f32(conv_w), perm, bblk, cre, cim, ab,
                         f32(rwkv_v0), f32(rwkv_v1), f32(rwkv_v2), v_first, tt)
        fa, fb, fc, fd = outs[:4]
        if l == 0:
            v_first = outs[4]
        x = _merge_call(l, x, mod4, f32(norm_mix_g), fa, fb, fc, fd, win, wa, wb, wc, glu, wo, tm)
        x = _ffn_call(l, x, mod4, f32(norm_ffn_g), w1, w2, fg, l == depth - 1, tm_ffn)
    return x.astype(in_dtype)
```

```python
import functools
import math

import numpy as np
import jax
import jax.numpy as jnp
from jax import lax
from jax.experimental import pallas as pl
from jax.experimental.pallas import tpu as pltpu

F32 = jnp.float32
BF16 = jnp.bfloat16

D_MODEL = 1024
N_BRANCH = 4
D_MIX = D_MODEL // N_BRANCH
HEAD_A = 64
H_A = D_MIX // HEAD_A
LORA_W = 32
LORA_A = 32
LORA_G = 64
LNX_EPS = 64e-5
CHUNK = 128
GROUP_B = 64
G_B = D_MIX // GROUP_B
CONV_K = 3
GROUP_D = 16
G_D = D_MIX // GROUP_D
N_STATE = 64
D_FF = 4 * D_MODEL
EPS = 1e-6
LN_EPS = 1e-5

A_COLS = 3 * D_MIX + LORA_W + LORA_A + LORA_G
B_COLS = 2 * D_MIX
C_COLS = 3 * D_MIX
D_COLS = D_MIX
OFF_B = A_COLS
OFF_C = OFF_B + B_COLS
OFF_D = OFF_C + C_COLS
OFF_G = OFF_D + D_COLS
LORA_COLS = LORA_W + LORA_A + LORA_G

WKV_CHUNK = 64
S5_STATES = G_D * N_STATE
S5_LANES = 2 * S5_STATES
S5_SUBSEQ = 8
S5_TILE = 256
FRONT_ROWS = 256
RW_COLS = 8 * D_MIX

(ROW_W0, ROW_A0, ROW_KK, ROW_KA, ROW_RK, ROW_LNXW, ROW_LNXB, ROW_SGW, ROW_SGB, ROW_S5D) = range(10)

VMEM_LIMIT = 56 * 1024 * 1024


def _cparams(sem):
    return pltpu.CompilerParams(dimension_semantics=sem, vmem_limit_bytes=VMEM_LIMIT)


def _dot(a, b):
    return jnp.dot(a.astype(BF16), b.astype(BF16), preferred_element_type=F32)


def _dot_nt(a, b):
    return lax.dot_general(a.astype(BF16), b.astype(BF16), (((1,), (1,)), ((), ())),
                           preferred_element_type=F32)


def _dot_tn(a, b):
    return lax.dot_general(a.astype(BF16), b.astype(BF16), (((0,), (0,)), ((), ())),
                           preferred_element_type=F32)


def _split3(x):
    hi = x.astype(BF16)
    r1 = x - hi.astype(F32)
    mid = r1.astype(BF16)
    lo = (r1 - mid.astype(F32)).astype(BF16)
    return hi, mid, lo


def _split2(x):
    hi = x.astype(BF16)
    return hi, (x - hi.astype(F32)).astype(BF16)


def _dot_sel(sel, x):
    hi, mid = _split2(x)
    return jnp.dot(sel, hi, preferred_element_type=F32) + jnp.dot(sel, mid, preferred_element_type=F32)


def _dot_x_sel(x, sel):
    hi, mid = _split2(x)
    return jnp.dot(hi, sel, preferred_element_type=F32) + jnp.dot(mid, sel, preferred_element_type=F32)


def _dot3(a, b):
    ah, am, _ = _split3(a)
    bh, bm, _ = _split3(b)
    return (jnp.dot(ah, bh, preferred_element_type=F32) + jnp.dot(ah, bm, preferred_element_type=F32)
            + jnp.dot(am, bh, preferred_element_type=F32))


def _sigmoid(x):
    return 0.5 * jnp.tanh(0.5 * x) + 0.5


def _gelu_tanh(x):
    return 0.5 * x * (1.0 + jnp.tanh(math.sqrt(2.0 / math.pi) * (x + 0.044715 * (x * x * x))))


def _rms_mod(x, g, sc, sh):
    ms = jnp.mean(x * x, axis=-1, keepdims=True)
    return x * lax.rsqrt(ms + EPS) * g * (1.0 + sc) + sh


def _layer_spec(a, l):
    return pl.BlockSpec((1,) + a.shape[1:], lambda b, t: (l,) + (0,) * (a.ndim - 1))


def _weight_spec(a, l):
    return pl.BlockSpec((1,) + a.shape[1:], lambda b, t: (l,) + (0,) * (a.ndim - 1),
                        pipeline_mode=pl.Buffered(1))


def _mod_spec(mod4, l):
    return pl.BlockSpec((1, 1) + mod4.shape[2:], lambda b, t: (l, b, 0, 0))


def _const_spec(a):
    return pl.BlockSpec(a.shape, lambda b, t: (0,) * a.ndim)


def _tok_spec(tt, cols):
    return pl.BlockSpec((1, tt, cols), lambda b, t: (b, t, 0))


def _ada_kernel(c_ref, w_ref, b_ref, o_ref):
    c = c_ref[...]
    ca = c * _sigmoid(c)
    o_ref[0] = _dot3(ca, w_ref[0]) + b_ref[0]


def _ada_call(c_pad, ada_w, ada_b):
    depth, d, n = ada_w.shape
    rows = c_pad.shape[0]
    bn = 1536
    return pl.pallas_call(
        _ada_kernel,
        grid=(depth, n // bn),
        in_specs=[
            pl.BlockSpec((rows, d), lambda l, j: (0, 0)),
            pl.BlockSpec((1, d, bn), lambda l, j: (l, 0, j)),
            pl.BlockSpec((1, 1, bn), lambda l, j: (l, 0, j)),
        ],
        out_specs=pl.BlockSpec((1, rows, bn), lambda l, j: (l, 0, j)),
        out_shape=jax.ShapeDtypeStruct((depth, rows, n), F32),
        compiler_params=_cparams(("parallel", "parallel")),
        name="ada_mod",
    )(c_pad, ada_w, ada_b.reshape(depth, 1, n))


def _wkv_part(tt, rw_s, lnx_w, lnx_b, o_ref, s_ref):
    n = WKV_CHUNK
    nc = tt // n

    row = lax.broadcasted_iota(jnp.int32, (n, n), 0)
    col = lax.broadcasted_iota(jnp.int32, (n, n), 1)
    ltri = jnp.where(row >= col, 1.0, 0.0).astype(BF16)
    strict = row > col
    incl = row >= col
    eye = jnp.where(row == col, 1.0, 0.0)
    same = lambda s: jnp.right_shift(row, s) == jnp.right_shift(col, s)

    items = [(ci, hd) for ci in range(nc) for hd in range(H_A)]
    head = lambda x, hd: x[:, hd * HEAD_A:(hd + 1) * HEAD_A]

    a_t, r_t, b_t, k_t, b_h, k_h, vv, g_end = [], [], [], [], [], [], [], []
    for ci in range(nc):
        rows = slice(ci * n, (ci + 1) * n)
        r = rw_s[rows, 0 * D_MIX:1 * D_MIX]
        lw = rw_s[rows, 1 * D_MIX:2 * D_MIX]
        k = rw_s[rows, 2 * D_MIX:3 * D_MIX]
        kk = rw_s[rows, 4 * D_MIX:5 * D_MIX]
        b = rw_s[rows, 5 * D_MIX:6 * D_MIX]
        e = _dot_sel(ltri, lw)
        eg = jnp.exp(e)
        ig = jnp.exp(-e)
        ge = eg[n - 1:n, :]
        a_t.append((-kk * jnp.exp(e - lw)).astype(BF16))
        r_t.append(r * eg)
        b_t.append((b * ig).astype(BF16))
        k_t.append((k * ig).astype(BF16))
        b_h.append((b * ig * ge).astype(BF16))
        k_h.append((k * ig * ge).astype(BF16))
        vv.append(rw_s[rows, 3 * D_MIX:4 * D_MIX].astype(BF16))
        g_end.append(ge)

    gm = [_dot_nt(jnp.concatenate([head(a_t[ci], hd), head(r_t[ci], hd).astype(BF16)], axis=0),
                  jnp.concatenate([head(b_t[ci], hd), head(k_t[ci], hd)], axis=0)) for ci, hd in items]
    a_ab = [jnp.where(strict, m[:n, :n], 0.0) for m in gm]
    a_ak = [jnp.where(strict, m[:n, n:], 0.0) for m in gm]
    a_rb = [jnp.where(incl, m[n:, :n], 0.0) for m in gm]
    a_rk = [jnp.where(incl, m[n:, n:], 0.0) for m in gm]
    inv = [eye + jnp.where(same(1), a, 0.0) for a in a_ab]
    for s in range(2, int(math.log2(n)) + 1):
        off_mask = same(s) & jnp.logical_not(same(s - 1))
        tmp = [_dot(t, jnp.where(off_mask, a, 0.0)) for t, a in zip(inv, a_ab)]
        inv = [t + _dot(x, t) for t, x in zip(inv, tmp)]
    akv = [_dot(a_ak[i], head(vv[ci], hd)) for i, (ci, hd) in enumerate(items)]
    pq = [_dot(inv[i], jnp.concatenate([head(a_t[ci], hd), akv[i].astype(BF16)], axis=1))
          for i, (ci, hd) in enumerate(items)]
    mn = [_dot_tn(pq[i], head(b_h[ci], hd)) for i, (ci, hd) in enumerate(items)]
    kv = [_dot_tn(head(vv[ci], hd), head(k_h[ci], hd)) for ci, hd in items]

    state = [s_ref[hd] for hd in range(H_A)]
    starts, rq, ark = [], [], []
    for ci in range(nc):
        starts.append(list(state))
        for hd in range(H_A):
            i = ci * H_A + hd
            s0 = state[hd]
            state[hd] = s0 * head(g_end[ci], hd) + _dot(s0, mn[i][:n, :]) + (mn[i][n:, :] + kv[i])
        for hd in range(H_A):
            i = ci * H_A + hd
            rq.append(_dot(a_rb[i], pq[i]))
            ark.append(_dot(a_rk[i], head(vv[ci], hd)))
    for hd in range(H_A):
        s_ref[hd] = state[hd]

    for ci in range(nc):
        outs = []
        for hd in range(H_A):
            i = ci * H_A + hd
            s0 = starts[ci][hd]
            ro = head(r_t[ci], hd) + rq[i][:, :n]
            o = _dot_nt(ro, s0) + rq[i][:, n:] + ark[i]
            mu = jnp.mean(o, axis=-1, keepdims=True)
            var = jnp.mean(jnp.square(o - mu), axis=-1, keepdims=True)
            outs.append((o - mu) * lax.rsqrt(var + LNX_EPS))
        rows = slice(ci * n, (ci + 1) * n)
        on = jnp.concatenate(outs, axis=1) * lnx_w + lnx_b
        g = rw_s[rows, 6 * D_MIX:7 * D_MIX]
        bonus = rw_s[rows, 7 * D_MIX:8 * D_MIX]
        o_ref[0, rows, :] = ((on + bonus) * g).astype(BF16)


def _s5_part(u5, perm, bblk, cre_ref, cim_ref, ab_re, ab_im, d_row, o_ref, bu_s, x_s, carry_s):
    half = S5_STATES
    steps = S5_TILE // S5_SUBSEQ
    nt = u5.shape[0] // S5_TILE

    u_p = [_dot_sel(perm, u5[n * S5_TILE:(n + 1) * S5_TILE, :]) for n in range(nt)]
    for n in range(nt):
        bu_s[n] = jnp.dot(u_p[n].astype(BF16), bblk, preferred_element_type=F32)

    a_re = jnp.broadcast_to(ab_re, (S5_SUBSEQ, half))
    a_im = jnp.broadcast_to(ab_im, (S5_SUBSEQ, half))

    def step(n, i, st):
        s_re, s_im = st
        rows = slice(i * S5_SUBSEQ, (i + 1) * S5_SUBSEQ)
        n_re = a_re * s_re - a_im * s_im + bu_s[n, rows, 0:half]
        n_im = a_re * s_im + a_im * s_re + bu_s[n, rows, half:]
        return n_re, n_im

    zero = jnp.zeros((S5_SUBSEQ, half), F32)
    st = [(zero, zero)] * nt
    for i in range(steps):
        st = [step(n, i, st[n]) for n in range(nt)]
    ends = st

    p_re, p_im = ab_re, ab_im
    for _ in range(int(math.log2(steps))):
        p_re, p_im = p_re * p_re - p_im * p_im, 2.0 * p_re * p_im

    c_re, c_im = carry_s[0:1, :], carry_s[1:2, :]
    st = []
    for n in range(nt):
        e_re, e_im = ends[n]
        in_re, in_im = [], []
        for j in range(S5_SUBSEQ):
            in_re.append(c_re)
            in_im.append(c_im)
            c_re, c_im = (p_re * c_re - p_im * c_im + e_re[j:j + 1, :],
                          p_re * c_im + p_im * c_re + e_im[j:j + 1, :])
        st.append((jnp.concatenate(in_re, axis=0), jnp.concatenate(in_im, axis=0)))
    carry_s[0:1, :] = c_re
    carry_s[1:2, :] = c_im

    for i in range(steps):
        st = [step(n, i, st[n]) for n in range(nt)]
        rows = slice(i * S5_SUBSEQ, (i + 1) * S5_SUBSEQ)
        for n in range(nt):
            x_s[n, rows, 0:half] = st[n][0]
            x_s[n, rows, half:] = st[n][1]

    for n in range(nt):
        y = (jnp.dot(x_s[n, :, 0:half].astype(BF16), cre_ref[0], preferred_element_type=F32)
             - jnp.dot(x_s[n, :, half:].astype(BF16), cim_ref[0], preferred_element_type=F32))
        f_p = _gelu_tanh(y + d_row * u_p[n]).astype(BF16)
        o_ref[0, n * S5_TILE:(n + 1) * S5_TILE, :] = lax.dot_general(
            perm, f_p, (((0,), (0,)), ((), ())), preferred_element_type=F32).astype(BF16)


def _mix_kernel(l, tt, *refs):
    has_vmix = l > 0
    refs = list(refs)
    (x_ref, mod_ref, ng_ref, win_ref, mu_ref, rows_ref, w2_ref, a2_ref, g2_ref, seg_ref, ws_ref, sgb_ref,
     cw_ref, perm_ref, bblk_ref, cre_ref, cim_ref, ab_ref) = refs[:18]
    refs = refs[18:]
    if has_vmix:
        vf_ref, v0_ref, v1_ref, v2_ref = refs[:4]
        refs = refs[4:]
        fa_ref, fb_ref, fc_ref, fd_ref = refs[:4]
        refs = refs[4:]
    else:
        fa_ref, fb_ref, fc_ref, fd_ref, vout_ref = refs[:5]
        refs = refs[5:]
    pa_s, z_s, rw_s, s_ref, bu_s, x_s, carry_s = refs
    prow = lambda i: rows_ref[i, l:l + 1, :]

    @pl.when(pl.program_id(1) == 0)
    def _():
        pa_s[0:8, :] = jnp.zeros((8, A_COLS), F32)
        z_s[0:8, :] = jnp.zeros((8, D_MIX), F32)
        s_ref[...] = jnp.zeros(s_ref.shape, F32)
        carry_s[...] = jnp.zeros(carry_s.shape, F32)

    mod = mod_ref[0, 0]

    def lora_w(w_ref, start):
        w = w_ref[0]
        parts = []
        if start:
            parts.append(jnp.zeros((start, D_MIX), F32))
        parts.append(w)
        if LORA_COLS - start - w.shape[0]:
            parts.append(jnp.zeros((LORA_COLS - start - w.shape[0], D_MIX), F32))
        return jnp.concatenate(parts, axis=0).astype(BF16)

    w2p, a2p, g2p = lora_w(w2_ref, 0), lora_w(a2_ref, LORA_W), lora_w(g2_ref, LORA_W + LORA_A)
    seg = seg_ref[...]
    row = lax.broadcasted_iota(jnp.int32, (CHUNK, CHUNK), 0)
    col = lax.broadcasted_iota(jnp.int32, (CHUNK, CHUNK), 1)
    wsm = [jnp.where(row >= col, ws_ref[0, gi], 0.0).astype(BF16) for gi in range(G_B)]

    u5_blocks = []
    for r0 in range(0, tt, FRONT_ROWS):
        rb = slice(r0, r0 + FRONT_ROWS)
        h = _rms_mod(x_ref[0, rb, :], ng_ref[l:l + 1, :], mod[1:2, :], mod[0:1, :])
        p = jnp.dot(h.astype(BF16), win_ref[0], preferred_element_type=F32)

        pa = p[:, :A_COLS]
        pa_s[8 + r0:8 + r0 + FRONT_ROWS, :] = pa
        prev = pa_s[7 + r0:7 + r0 + FRONT_ROWS, :]
        pa = pa + (prev - pa) * mu_ref[l:l + 1, :]
        r = pa[:, 0:D_MIX]
        k = pa[:, D_MIX:2 * D_MIX]
        v = pa[:, 2 * D_MIX:3 * D_MIX]
        lora = pa[:, 3 * D_MIX:A_COLS]
        lw = -math.exp(-0.5) * _sigmoid(prow(ROW_W0) + _dot(jnp.tanh(lora), w2p))
        if has_vmix:
            vgate = _sigmoid(v0_ref[l - 1:l, :] + _dot(_dot(v, v1_ref[0]), v2_ref[0]))
            v = v + (vf_ref[0, rb, :] - v) * vgate
        else:
            vout_ref[0, rb, :] = v
        a = _sigmoid(prow(ROW_A0) + _dot(lora, a2p))
        g = _dot(_sigmoid(lora), g2p)
        kk = k * prow(ROW_KK)
        kk_norm = jnp.sqrt(_dot_x_sel(kk * kk, seg))
        kk = kk / jnp.maximum(kk_norm, 1e-12)
        k = k * (1.0 + (a - 1.0) * prow(ROW_KA))
        bonus = _dot_x_sel(r * k * prow(ROW_RK), seg) * v
        rw_s[rb, 0 * D_MIX:1 * D_MIX] = r
        rw_s[rb, 1 * D_MIX:2 * D_MIX] = lw
        rw_s[rb, 2 * D_MIX:3 * D_MIX] = k
        rw_s[rb, 3 * D_MIX:4 * D_MIX] = v
        rw_s[rb, 4 * D_MIX:5 * D_MIX] = kk
        rw_s[rb, 5 * D_MIX:6 * D_MIX] = kk * a
        rw_s[rb, 6 * D_MIX:7 * D_MIX] = g
        rw_s[rb, 7 * D_MIX:8 * D_MIX] = bonus

        z = _gelu_tanh(p[:, OFF_B:OFF_C])
        su = z[:, :D_MIX]
        sv = z[:, D_MIX:]
        mu_v = jnp.mean(sv, axis=-1, keepdims=True)
        var_v = jnp.mean(jnp.square(sv - mu_v), axis=-1, keepdims=True)
        sv = (sv - mu_v) * lax.rsqrt(var_v + LN_EPS) * prow(ROW_SGW) + prow(ROW_SGB)
        sv_b = sv.astype(BF16)
        for n in range(FRONT_ROWS // CHUNK):
            rows = slice(n * CHUNK, (n + 1) * CHUNK)
            mixed = jnp.concatenate(
                [jnp.dot(wsm[gi], sv_b[rows, gi * GROUP_B:(gi + 1) * GROUP_B], preferred_element_type=F32)
                 for gi in range(G_B)], axis=1) + sgb_ref[0]
            fb_ref[0, r0 + n * CHUNK:r0 + (n + 1) * CHUNK, :] = (su[rows, :] * mixed).astype(BF16)

        pc = p[:, OFF_C:OFF_D]
        bg = pc[:, :D_MIX]
        zc = pc[:, D_MIX:2 * D_MIX] * pc[:, 2 * D_MIX:]
        z_s[8 + r0:8 + r0 + FRONT_ROWS, :] = zc
        y = (cw_ref[l, 0:1, :] * z_s[6 + r0:6 + r0 + FRONT_ROWS, :]
             + cw_ref[l, 1:2, :] * z_s[7 + r0:7 + r0 + FRONT_ROWS, :] + cw_ref[l, 2:3, :] * zc)
        fc_ref[0, rb, :] = (bg * y).astype(BF16)

        u5_blocks.append(p[:, OFF_D:OFF_G])

    pa_s[0:8, :] = pa_s[tt:tt + 8, :]
    z_s[0:8, :] = z_s[tt:tt + 8, :]
    u5 = jnp.concatenate(u5_blocks, axis=0)

    _wkv_part(tt, rw_s, prow(ROW_LNXW), prow(ROW_LNXB), fa_ref, s_ref)

    _s5_part(u5, perm_ref[...], bblk_ref[0], cre_ref, cim_ref, ab_ref[0, 0:1, :], ab_ref[0, 1:2, :],
             prow(ROW_S5D), fd_ref, bu_s, x_s, carry_s)


def _mix_call(l, x, mod4, ng, win, mu, rows, w2, a2, g2, seg, ws, sgb, cw, perm, bblk, cre, cim, ab,
              v0, v1, v2, v_first, tt):
    bsz, seq, d = x.shape
    has_vmix = l > 0
    ins = [x, mod4, ng, win, mu, rows, w2, a2, g2, seg, ws, sgb, cw, perm, bblk, cre, cim, ab]
    in_specs = [
        _tok_spec(tt, d), _mod_spec(mod4, l), _const_spec(ng),
        pl.BlockSpec((1, d, OFF_G), lambda b, t: (l, 0, 0)),
        _const_spec(mu), _const_spec(rows), _layer_spec(w2, l), _layer_spec(a2, l), _layer_spec(g2, l),
        _const_spec(seg), _layer_spec(ws, l), _layer_spec(sgb, l), _const_spec(cw), _const_spec(perm),
        _layer_spec(bblk, l), _layer_spec(cre, l), _layer_spec(cim, l), _layer_spec(ab, l),
    ]
    feat = jax.ShapeDtypeStruct((bsz, seq, D_MIX), BF16)
    out_shape = [feat, feat, feat, feat]
    out_specs = [_tok_spec(tt, D_MIX)] * 4
    if has_vmix:
        ins += [v_first, v0, v1, v2]
        in_specs += [_tok_spec(tt, D_MIX), _const_spec(v0), _layer_spec(v1, l - 1), _layer_spec(v2, l - 1)]
    else:
        out_shape.append(jax.ShapeDtypeStruct((bsz, seq, D_MIX), F32))
        out_specs.append(_tok_spec(tt, D_MIX))
    return pl.pallas_call(
        functools.partial(_mix_kernel, l, tt),
        grid=(bsz, seq // tt),
        in_specs=in_specs,
        out_specs=out_specs,
        out_shape=out_shape,
        scratch_shapes=[pltpu.VMEM((tt + 8, A_COLS), F32), pltpu.VMEM((tt + 8, D_MIX), F32),
                        pltpu.VMEM((tt, RW_COLS), F32), pltpu.VMEM((H_A, HEAD_A, HEAD_A), F32),
                        pltpu.VMEM((tt // S5_TILE, S5_TILE, S5_LANES), F32),
                        pltpu.VMEM((tt // S5_TILE, S5_TILE, S5_LANES), F32),
                        pltpu.VMEM((8, S5_STATES), F32)],
        compiler_params=_cparams(("parallel", "arbitrary")),
        name="mix",
    )(*ins)


def _s5_param_kernel(are_ref, aim_ref, ldt_ref, bre_ref, bim_ref, abre_ref, abim_ref, bbre_ref, bbim_ref):
    lam_re = jnp.minimum(are_ref[...], -1e-4)
    lam_im = aim_ref[...]
    dt = jnp.exp(ldt_ref[...])
    mag = jnp.exp(lam_re * dt)
    ab_re = mag * jnp.cos(lam_im * dt)
    ab_im = mag * jnp.sin(lam_im * dt)
    den = lam_re * lam_re + lam_im * lam_im
    q_re = ((ab_re - 1.0) * lam_re + ab_im * lam_im) / den
    q_im = (ab_im * lam_re - (ab_re - 1.0) * lam_im) / den
    abre_ref[...] = ab_re
    abim_ref[...] = ab_im
    b_re = bre_ref[...]
    b_im = bim_ref[...]
    bbre_ref[...] = q_re * b_re - q_im * b_im
    bbim_ref[...] = q_re * b_im + q_im * b_re


def _s5_param_call(a_re, a_im, log_dt, b_re, b_im):
    rows = a_re.size
    col = lambda a: a.reshape(rows, 1).astype(F32)
    ldt = jnp.repeat(log_dt.astype(F32), N_STATE, axis=-1).reshape(rows, 1)
    return pl.pallas_call(
        _s5_param_kernel,
        out_shape=(jax.ShapeDtypeStruct((rows, 1), F32), jax.ShapeDtypeStruct((rows, 1), F32),
                   jax.ShapeDtypeStruct((rows, GROUP_D), F32), jax.ShapeDtypeStruct((rows, GROUP_D), F32)),
        name="s5_params",
    )(col(a_re), col(a_im), ldt, b_re.reshape(rows, GROUP_D).astype(F32),
      b_im.reshape(rows, GROUP_D).astype(F32))


def _merge_kernel(l, x_ref, mod_ref, ng_ref, fa_ref, fb_ref, fc_ref, fd_ref, wg_ref, wa_ref, wb_ref, wc_ref,
                  glu_ref, wo_ref, o_ref):
    d = D_MODEL
    mod = mod_ref[0, 0]
    x = x_ref[0]
    hb = _rms_mod(x, ng_ref[l:l + 1, :], mod[1:2, :], mod[0:1, :]).astype(BF16)

    half = d // 2
    acc = jnp.zeros(x.shape, F32)
    for c0 in range(0, d, half):
        cs = slice(c0, c0 + half)

        def gate(i):
            cols = slice(OFF_G + i * d + c0, OFF_G + i * d + c0 + half)
            return _sigmoid(jnp.dot(hb, wg_ref[0, :, cols], preferred_element_type=F32))

        merged = gate(0) * jnp.dot(fa_ref[0], wa_ref[0, :, cs], preferred_element_type=F32)
        merged += gate(1) * jnp.dot(fb_ref[0], wb_ref[0, :, cs], preferred_element_type=F32)
        merged += gate(2) * jnp.dot(fc_ref[0], wc_ref[0, :, cs], preferred_element_type=F32)
        hv = jnp.dot(fd_ref[0], glu_ref[0, :, c0:c0 + half], preferred_element_type=F32)
        hg = jnp.dot(fd_ref[0], glu_ref[0, :, d + c0:d + c0 + half], preferred_element_type=F32)
        merged += gate(3) * (hv * _sigmoid(hg))
        acc += jnp.dot(merged.astype(BF16), wo_ref[0, cs, :], preferred_element_type=F32)
    o_ref[0] = x + mod[2:3, :] * acc


def _merge_call(l, x, mod4, ng, fa, fb, fc, fd, wg, wa, wb, wc, glu, wo, tm):
    bsz, seq, d = x.shape
    feat = _tok_spec(tm, D_MIX)
    return pl.pallas_call(
        functools.partial(_merge_kernel, l),
        grid=(bsz, seq // tm),
        in_specs=[_tok_spec(tm, d), _mod_spec(mod4, l), _const_spec(ng), feat, feat, feat, feat,
                  _weight_spec(wg, l), _layer_spec(wa, l), _layer_spec(wb, l), _layer_spec(wc, l),
                  _weight_spec(glu, l), _weight_spec(wo, l)],
        out_specs=_tok_spec(tm, d),
        out_shape=jax.ShapeDtypeStruct((bsz, seq, d), F32),
        compiler_params=_cparams(("parallel", "parallel")),
        name="merge",
    )(x, mod4, ng, fa, fb, fc, fd, wg, wa, wb, wc, glu, wo)


def _ffn_kernel(l, final, x_ref, mod_ref, ng_ref, w1_ref, w2_ref, fg_ref, o_ref):
    mod = mod_ref[0, 0]
    x = x_ref[0]
    hb = _rms_mod(x, ng_ref[l:l + 1, :], mod[4:5, :], mod[3:4, :]).astype(BF16)
    acc = jnp.zeros(x.shape, F32)
    step = D_MODEL
    for j in range(D_FF // step):
        a = jnp.dot(hb, w1_ref[0, :, j * step:(j + 1) * step], preferred_element_type=F32)
        a = jnp.square(jnp.maximum(a, 0.0))
        acc += jnp.dot(a.astype(BF16), w2_ref[0, j * step:(j + 1) * step, :], preferred_element_type=F32)
    y = x + mod[5:6, :] * acc
    if final:
        ms = jnp.mean(y * y, axis=-1, keepdims=True)
        y = y * lax.rsqrt(ms + EPS) * fg_ref[...]
    o_ref[0] = y


def _ffn_call(l, x, mod4, ng, w1, w2, fg, final, tm):
    bsz, seq, d = x.shape
    return pl.pallas_call(
        functools.partial(_ffn_kernel, l, final),
        grid=(bsz, seq // tm),
        in_specs=[_tok_spec(tm, d), _mod_spec(mod4, l), _const_spec(ng), _weight_spec(w1, l),
                  _weight_spec(w2, l), _const_spec(fg)],
        out_specs=_tok_spec(tm, d),
        out_shape=jax.ShapeDtypeStruct((bsz, seq, d), F32),
        compiler_params=_cparams(("parallel", "parallel")),
        name="ffn",
    )(x, mod4, ng, w1, w2, fg)


def _tile(seq, want):
    return want if seq % want == 0 else seq


def kernel(x, c, ada_w, ada_b, norm_mix_g, w_in, rwkv_mu, rwkv_w0, rwkv_w2, rwkv_a0, rwkv_a2, rwkv_g2,
           rwkv_v0, rwkv_v1, rwkv_v2, rwkv_kk, rwkv_ka, rwkv_rk, rwkv_lnx_w, rwkv_lnx_b, rwkv_out,
           sg_ln_w, sg_ln_b, sg_ws, sg_bs, sg_out, conv_w, conv_out, s5_a_re, s5_a_im, s5_b_re, s5_b_im,
           s5_c_re, s5_c_im, s5_d, s5_log_dt, s5_glu_w, w_o, norm_ffn_g, ffn_w1, ffn_w2, final_g):
    in_dtype = x.dtype
    bsz, seq, d = x.shape
    depth = ada_w.shape[0]
    x = x.astype(F32)

    tt = _tile(seq, 512)
    tm = _tile(seq, 512)
    tm_ffn = _tile(seq, 1024)

    c_rows = 16
    c_pad = jnp.pad(c.astype(F32), ((0, c_rows - bsz), (0, 0)))
    mod4 = _ada_call(c_pad, ada_w.astype(F32), ada_b.astype(F32)).reshape(depth, c_rows, 6, d)

    rows = jnp.stack([rwkv_w0, rwkv_a0, rwkv_kk, rwkv_ka, rwkv_rk.reshape(depth, D_MIX), rwkv_lnx_w,
                      rwkv_lnx_b, sg_ln_w, sg_ln_b, s5_d], axis=0).astype(F32)
    sgb = jnp.repeat(jnp.swapaxes(sg_bs, 1, 2), GROUP_B, axis=2).astype(F32)
    win = w_in.astype(BF16)
    wa, wb, wc = rwkv_out.astype(BF16), sg_out.astype(BF16), conv_out.astype(BF16)
    glu = s5_glu_w.astype(BF16)
    wo = w_o.astype(BF16)
    w1 = ffn_w1.astype(BF16)
    w2 = ffn_w2.astype(BF16)
    fg = final_g.reshape(1, d).astype(F32)

    head_id = np.arange(D_MIX) // HEAD_A
    seg = jnp.asarray(head_id[:, None] == head_id[None, :], BF16)
    steps = S5_TILE // S5_SUBSEQ
    dst = np.arange(S5_TILE)
    src = (dst % S5_SUBSEQ) * steps + dst // S5_SUBSEQ
    perm = jnp.asarray(src[:, None] == np.arange(S5_TILE)[None, :], BF16)

    abre, abim, bbre, bbim = _s5_param_call(s5_a_re, s5_a_im, s5_log_dt, s5_b_re, s5_b_im)
    ab = jnp.concatenate([abre.reshape(depth, 1, S5_STATES), abim.reshape(depth, 1, S5_STATES)], axis=1)
    eye_g = jnp.asarray(np.eye(G_D), F32)

    def in_blk(bb):
        t = bb.reshape(depth, G_D, N_STATE, GROUP_D)
        return jnp.einsum('lgnc,gh->lgchn', t, eye_g).reshape(depth, D_MIX, S5_STATES)

    def out_blk(cc):
        return jnp.einsum('lgcn,gh->lgnhc', cc.astype(F32), eye_g).reshape(depth, S5_STATES, D_MIX).astype(BF16)

    bblk = jnp.concatenate([in_blk(bbre), in_blk(bbim)], axis=2).astype(BF16)
    cre = out_blk(s5_c_re)
    cim = out_blk(s5_c_im)

    f32 = lambda a: a.astype(F32)
    v_first = None
    for l in range(depth):
        outs = _mix_call(l, x, mod4, f32(norm_mix_g), win, f32(rwkv_mu), rows, f32(rwkv_w2), f32(rwkv_a2),
                         f32(rwkv_g2), seg, f32(sg_ws), sgb, f32(conv_w), perm, bblk, cre, cim, ab,
                         f32(rwkv_v0), f32(rwkv_v1), f32(rwkv_v2), v_first, tt)
        fa, fb, fc, fd = outs[:4]
        if l == 0:
            v_first = outs[4]
        x = _merge_call(l, x, mod4, f32(norm_mix_g), fa, fb, fc, fd, win, wa, wb, wc, glu, wo, tm)
        x = _ffn_call(l, x, mod4, f32(norm_ffn_g), w1, w2, fg, l == depth - 1, tm_ffn)
    return x.astype(in_dtype)
```
